```python
import math
import jax, jax.numpy as jnp
from jax import lax
import numpy as np

D_MODEL = 1024
BATCH = 8
SEQ = 4096
DEPTH = 1

CHUNK = 64
PLE_DIM = 256
D_MIX = D_MODEL
CONV_WIDTH = D_MIX // 2
CONV_GROUPS = 8
CONV_K = 3
ATTN_WIDTH = D_MIX - CONV_WIDTH
N_HEADS = 4
HEAD_DIM = ATTN_WIDTH // (2 * N_HEADS)
V_DIM = 2 * HEAD_DIM
ROT_DIM = HEAD_DIM // 4
ROPE_THETA = 500000.0
Q_BLOCK = 128
EPS = 1e-6
SUBLN_EPS = 1e-5
SPLIT_SIZES = [CONV_WIDTH] * 4 + [ATTN_WIDTH] * 4
IN_COLS = sum(SPLIT_SIZES)

kernel_name = "hymba_conv_diffattn_ple_block"


def rmsnorm(x, g, eps=EPS):
    xf = x.astype(jnp.float32)
    y = xf * lax.rsqrt(jnp.mean(xf * xf, axis=-1, keepdims=True) + eps) * g.astype(jnp.float32)
    return y.astype(x.dtype)


def causal_depthwise_conv(u, w, b):
    c = u.shape[-1]
    y = lax.conv_general_dilated(
        u, w.astype(u.dtype)[:, None, :], window_strides=(1,),
        padding=[(CONV_K - 1, 0)], dimension_numbers=('NWC', 'WIO', 'NWC'),
        feature_group_count=c)
    return y + b.astype(u.dtype)


def partial_rope(x, positions):
    half = ROT_DIM // 2
    inv_freq = ROPE_THETA ** (-jnp.arange(half, dtype=jnp.float32) / half)
    ang = positions.astype(jnp.float32)[:, :, None] * inv_freq
    cos = jnp.cos(ang)[:, :, None, None, :]
    sin = jnp.sin(ang)[:, :, None, None, :]
    x1 = x[..., :half]
    x2 = x[..., half:ROT_DIM]
    rest = x[..., ROT_DIM:]
    return jnp.concatenate([x1 * cos - x2 * sin, x2 * cos + x1 * sin, rest], axis=-1)


def diff_attention(q, k, v, lam, positions):
    b, s = q.shape[0], q.shape[1]
    nb = s // Q_BLOCK
    qf = partial_rope(q.astype(jnp.float32), positions) * (HEAD_DIM ** -0.5)
    kf = partial_rope(k.astype(jnp.float32), positions)
    vf = v.astype(jnp.float32)
    q_blocks = qf.reshape(b, nb, Q_BLOCK, N_HEADS, 2, HEAD_DIM).swapaxes(0, 1)
    q_chunk = (jnp.arange(s) // CHUNK).reshape(nb, Q_BLOCK)
    k_chunk = jnp.arange(s) // CHUNK

    def one_block(args):
        qb, qc = args
        scores = jnp.einsum('bqhmd,bkhmd->bhmqk', qb, kf)
        mask = k_chunk[None, :] <= qc[:, None]
        scores = jnp.where(mask, scores, -jnp.inf)
        probs = jax.nn.softmax(scores, axis=-1)
        w = probs[:, :, 0] - lam * probs[:, :, 1]
        return jnp.einsum('bhqk,bkhd->bqhd', w, vf)

    out = lax.map(one_block, (q_blocks, q_chunk))
    return out.swapaxes(0, 1).reshape(b, s, N_HEADS, V_DIM)


def setup_inputs(seed: int = 0) -> dict:
    key = jax.random.key(seed)
    ks = jax.random.split(key, 20)
    f32 = jnp.float32
    x = jax.random.normal(ks[0], (BATCH, SEQ, D_MODEL), f32)
    p = jax.random.normal(ks[1], (DEPTH, BATCH, SEQ, PLE_DIM), f32)
    offsets = jax.random.randint(ks[2], (BATCH, 1), 0, 64, dtype=jnp.int32) * CHUNK
    positions = (offsets + jnp.arange(SEQ, dtype=jnp.int32)[None, :]).astype(jnp.int32)
    norm_mix = 1.0 + 0.02 * jax.random.normal(ks[3], (DEPTH, D_MODEL), f32)
    w_in = jax.random.normal(ks[4], (DEPTH, D_MODEL, IN_COLS), f32) * D_MODEL ** -0.5
    conv_w = jax.random.normal(ks[5], (DEPTH, CONV_K, CONV_WIDTH), f32) * CONV_K ** -0.5
    conv_b = 0.01 * jax.random.normal(ks[6], (DEPTH, CONV_WIDTH), f32)
    lambda_q1 = 0.1 * jax.random.normal(ks[7], (DEPTH, HEAD_DIM), f32)
    lambda_k1 = 0.1 * jax.random.normal(ks[8], (DEPTH, HEAD_DIM), f32)
    lambda_q2 = 0.1 * jax.random.normal(ks[9], (DEPTH, HEAD_DIM), f32)
    lambda_k2 = 0.1 * jax.random.normal(ks[10], (DEPTH, HEAD_DIM), f32)
    subln_g = 1.0 + 0.02 * jax.random.normal(ks[11], (DEPTH, V_DIM), f32)
    w_out = jax.random.normal(ks[12], (DEPTH, D_MIX, D_MODEL), f32) * D_MIX ** -0.5
    norm_ple = 1.0 + 0.02 * jax.random.normal(ks[13], (DEPTH, D_MODEL), f32)
    w_ple_gate = jax.random.normal(ks[14], (DEPTH, D_MODEL, D_MODEL), f32) * D_MODEL ** -0.5
    w_ple_proj = jax.random.normal(ks[15], (DEPTH, PLE_DIM, D_MODEL), f32) * PLE_DIM ** -0.5
    final_norm = 1.0 + 0.02 * jax.random.normal(ks[16], (D_MODEL,), f32)
    return {"x": x, "p": p, "positions": positions, "norm_mix": norm_mix, "w_in": w_in,
            "conv_w": conv_w, "conv_b": conv_b, "lambda_q1": lambda_q1, "lambda_k1": lambda_k1,
            "lambda_q2": lambda_q2, "lambda_k2": lambda_k2, "subln_g": subln_g, "w_out": w_out,
            "norm_ple": norm_ple, "w_ple_gate": w_ple_gate, "w_ple_proj": w_ple_proj,
            "final_norm": final_norm}


def reference(x, p, positions, norm_mix, w_in, conv_w, conv_b, lambda_q1, lambda_k1,
              lambda_q2, lambda_k2, subln_g, w_out, norm_ple, w_ple_gate, w_ple_proj,
              final_norm):
    b, s, _ = x.shape
    split_idx = np.cumsum(SPLIT_SIZES)[:-1].tolist()
    h = x
    for i in range(DEPTH):
        lam_init = 0.8 - 0.6 * math.exp(-0.3 * i)
        u = rmsnorm(h, norm_mix[i])
        proj = u @ w_in[i]
        cx, cb, cc, cz, q, k, v, az = jnp.split(proj, split_idx, axis=-1)

        y_conv = cb * causal_depthwise_conv(cc * cx, conv_w[i], conv_b[i]) * jax.nn.silu(cz)

        lam = (jnp.exp(jnp.sum(lambda_q1[i].astype(jnp.float32) * lambda_k1[i].astype(jnp.float32)))
               - jnp.exp(jnp.sum(lambda_q2[i].astype(jnp.float32) * lambda_k2[i].astype(jnp.float32)))
               + lam_init)
        q = q.reshape(b, s, N_HEADS, 2, HEAD_DIM)
        k = k.reshape(b, s, N_HEADS, 2, HEAD_DIM)
        v = v.reshape(b, s, N_HEADS, V_DIM)
        o = diff_attention(q, k, v, lam, positions)
        o = rmsnorm(o, subln_g[i], SUBLN_EPS) * (1.0 - lam_init)
        y_attn = o.reshape(b, s, ATTN_WIDTH).astype(h.dtype) * jax.nn.silu(az)

        h = h + jnp.concatenate([y_conv, y_attn], axis=-1) @ w_out[i]

        gate = jax.nn.sigmoid(rmsnorm(h, norm_ple[i]) @ w_ple_gate[i])
        h = h + gate * (p[i].astype(h.dtype) @ w_ple_proj[i])
    return rmsnorm(h, final_norm)
```

```python
import functools
import math

import jax
import jax.numpy as jnp
from jax import lax
from jax.experimental import pallas as pl
from jax.experimental.pallas import tpu as pltpu

D_MODEL = 1024
CHUNK = 64
PLE_DIM = 256
CONV_WIDTH = 512
CONV_K = 3
ATTN_WIDTH = 512
N_HEADS = 4
HEAD_DIM = 64
V_DIM = 2 * HEAD_DIM
ROT_DIM = HEAD_DIM // 4
ROPE_THETA = 500000.0
EPS = 1e-6
SUBLN_EPS = 1e-5
GROUP = 512
LANES = 128
SUBLANES = 8
NEG = -1e30

PROJ_TM = 512
ATTN_TQ = 256
ATTN_TK = 256
OUT_TM = 512
VMEM_LIMIT = 56 * 1024 * 1024

_NT = (((1,), (1,)), ((), ()))


def _const_spec(shape):
    return pl.BlockSpec(shape, lambda *_: (0,) * len(shape), pipeline_mode=pl.Buffered(1))


def _proj_kernel(x_ref, pos_ref, g_ref, w_ref, wvt_ref, cw_ref, cb_ref, invf_ref, sgn_ref,
                 yconv_ref, q_ref, k_ref, vt_ref, gate_ref, ubuf_ref, *, tm):
    si = pl.program_id(1)
    x = x_ref[0]
    ms = jnp.mean(x * x, axis=-1, keepdims=True)
    u = (x * lax.rsqrt(ms + EPS) * g_ref[...]).astype(jnp.bfloat16)

    def proj(c):
        return jnp.dot(u, w_ref[:, c * GROUP:(c + 1) * GROUP], preferred_element_type=jnp.float32)

    uc = proj(2) * proj(0)

    @pl.when(si == 0)
    def _():
        ubuf_ref[0:SUBLANES, :] = jnp.zeros((SUBLANES, GROUP), jnp.float32)

    @pl.when(si > 0)
    def _():
        ubuf_ref[0:SUBLANES, :] = ubuf_ref[tm:tm + SUBLANES, :]

    ubuf_ref[SUBLANES:tm + SUBLANES, :] = uc
    u1 = ubuf_ref[SUBLANES - 1:tm + SUBLANES - 1, :]
    u2 = ubuf_ref[SUBLANES - 2:tm + SUBLANES - 2, :]
    conv = cw_ref[0:1, :] * u2 + cw_ref[1:2, :] * u1 + cw_ref[2:3, :] * uc + cb_ref[...]
    cz = proj(3)
    yconv_ref[0] = (proj(1) * conv * (cz * jax.nn.sigmoid(cz))).astype(jnp.bfloat16)

    ang = pos_ref[0].astype(jnp.float32) * invf_ref[...]
    cos = jnp.cos(ang)
    sin = jnp.sin(ang) * sgn_ref[...]
    lane = lax.broadcasted_iota(jnp.int32, (1, LANES), 1)
    low = (lane % HEAD_DIM) < (ROT_DIM // 2)

    def rope(t, scale):
        outs = []
        for h in range(N_HEADS):
            th = t[:, h * LANES:(h + 1) * LANES]
            partner = jnp.where(low, pltpu.roll(th, LANES - ROT_DIM // 2, 1),
                                pltpu.roll(th, ROT_DIM // 2, 1))
            r = th * cos + partner * sin
            outs.append(r * scale if scale != 1.0 else r)
        return jnp.concatenate(outs, axis=1)

    q_ref[0] = rope(proj(4), HEAD_DIM ** -0.5).astype(jnp.bfloat16)
    k_ref[0] = rope(proj(5), 1.0).astype(jnp.bfloat16)
    vt_ref[0] = lax.dot_general(wvt_ref[...], u, _NT,
                                preferred_element_type=jnp.float32).astype(jnp.bfloat16)
    az = proj(7)
    gate_ref[0] = (az * jax.nn.sigmoid(az)).astype(jnp.bfloat16)


def _projection(x, pos3, g, w_bf, wvt_bf, cw, cb, invf, sgn):
    b, s, d = x.shape
    tm = PROJ_TM
    tok = lambda bi, si: (bi, si, 0)
    out_tok = jax.ShapeDtypeStruct((b, s, GROUP), jnp.bfloat16)
    return pl.pallas_call(
        functools.partial(_proj_kernel, tm=tm),
        grid=(b, s // tm),
        in_specs=[
            pl.BlockSpec((1, tm, d), tok),
            pl.BlockSpec((1, tm, 1), tok),
            _const_spec((1, d)),
            _const_spec(w_bf.shape),
            _const_spec(wvt_bf.shape),
            _const_spec(cw.shape),
            _const_spec(cb.shape),
            _const_spec(invf.shape),
            _const_spec(sgn.shape),
        ],
        out_specs=[
            pl.BlockSpec((1, tm, GROUP), tok),
            pl.BlockSpec((1, tm, GROUP), tok),
            pl.BlockSpec((1, tm, GROUP), tok),
            pl.BlockSpec((1, GROUP, tm), lambda bi, si: (bi, 0, si)),
            pl.BlockSpec((1, tm, GROUP), tok),
        ],
        out_shape=[out_tok, out_tok, out_tok,
                   jax.ShapeDtypeStruct((b, GROUP, s), jnp.bfloat16), out_tok],
        scratch_shapes=[pltpu.VMEM((tm + SUBLANES, GROUP), jnp.float32)],
        compiler_params=pltpu.CompilerParams(
            dimension_semantics=("parallel", "arbitrary"), vmem_limit_bytes=VMEM_LIMIT),
        name="proj_conv_rope",
    )(x, pos3, g, w_bf, wvt_bf, cw, cb, invf, sgn)


def _attn_kernel(q_ref, k_ref, vt_ref, gate_ref, lam_ref, g_ref, y_ref, acc_ref,
                 *, tq, tk, lam_init):
    qi = pl.program_id(1)
    lp = lam_ref[...]
    lam = (jnp.exp(jnp.sum(lp[0:1] * lp[1:2], axis=1, keepdims=True))
           - jnp.exp(jnp.sum(lp[2:3] * lp[3:4], axis=1, keepdims=True)) + lam_init)
    lane = lax.broadcasted_iota(jnp.int32, (1, LANES), 1)
    kchunk = lax.broadcasted_iota(jnp.int32, (tk, tq), 0) // CHUNK
    qchunk = lax.broadcasted_iota(jnp.int32, (tk, tq), 1) // CHUNK
    diag_mask = kchunk <= qchunk

    for h in range(N_HEADS):
        hs = slice(h * LANES, (h + 1) * LANES)
        qh = q_ref[0, :, hs]
        zero = jnp.zeros_like(qh)
        qz = (jnp.where(lane < HEAD_DIM, qh, zero), jnp.where(lane >= HEAD_DIM, qh, zero))

        def step(ki, carry, masked):
            off = pl.multiple_of(ki * tk, tk)
            kh = k_ref[0, pl.ds(off, tk), hs]
            vh = vt_ref[0, hs, pl.ds(off, tk)]
            new = []
            for m in range(2):
                m_old, l_old = carry[2 * m], carry[2 * m + 1]
                st = lax.dot_general(kh, qz[m], _NT, preferred_element_type=jnp.float32)
                if masked:
                    st = jnp.where(diag_mask, st, NEG)
                m_new = jnp.maximum(m_old, jnp.max(st, axis=0, keepdims=True))
                alpha = jnp.exp(m_old - m_new)
                p = jnp.exp(st - m_new)
                l_new = alpha * l_old + jnp.sum(p, axis=0, keepdims=True)
                acc_ref[m] = alpha * acc_ref[m] + jnp.dot(
                    vh, p.astype(jnp.bfloat16), preferred_element_type=jnp.float32)
                new += [m_new, l_new]
            return tuple(new)

        acc_ref[...] = jnp.zeros_like(acc_ref)
        m0 = jnp.full((1, tq), NEG, jnp.float32)
        l0 = jnp.zeros((1, tq), jnp.float32)
        carry = lax.fori_loop(0, qi, lambda ki, c: step(ki, c, False), (m0, l0, m0, l0))
        _, l1, _, l2 = step(qi, carry, True)

        o = acc_ref[0] * (1.0 / l1) - lam * (acc_ref[1] * (1.0 / l2))
        on = o * lax.rsqrt(jnp.mean(o * o, axis=0, keepdims=True) + SUBLN_EPS)
        y = on.T * (g_ref[...] * (1.0 - lam_init)) * gate_ref[0, :, hs].astype(jnp.float32)
        y_ref[0, :, hs] = y.astype(jnp.bfloat16)


def _attention(q, k, vt, gate, lam_params, subln_g, lam_init):
    b, s, w = q.shape
    tq, tk = ATTN_TQ, ATTN_TK
    assert tq == tk and tq % CHUNK == 0
    qtile = lambda bi, qi: (bi, qi, 0)
    return pl.pallas_call(
        functools.partial(_attn_kernel, tq=tq, tk=tk, lam_init=lam_init),
        grid=(b, s // tq),
        in_specs=[
            pl.BlockSpec((1, tq, w), qtile),
            pl.BlockSpec((1, s, w), lambda bi, qi: (bi, 0, 0)),
            pl.BlockSpec((1, w, s), lambda bi, qi: (bi, 0, 0)),
            pl.BlockSpec((1, tq, w), qtile),
            _const_spec(lam_params.shape),
            _const_spec(subln_g.shape),
        ],
        out_specs=pl.BlockSpec((1, tq, w), qtile),
        out_shape=jax.ShapeDtypeStruct((b, s, w), jnp.bfloat16),
        scratch_shapes=[pltpu.VMEM((2, V_DIM, tq), jnp.float32)],
        compiler_params=pltpu.CompilerParams(
            dimension_semantics=("parallel", "arbitrary"), vmem_limit_bytes=VMEM_LIMIT),
        name="diff_attention",
    )(q, k, vt, gate, lam_params, subln_g)


def _rms(h, g):
    return h * lax.rsqrt(jnp.mean(h * h, axis=-1, keepdims=True) + EPS) * g


def _out_kernel(x_ref, yc_ref, ya_ref, p_ref, wo_ref, np_ref, wg_ref, wp_ref, fn_ref, o_ref):
    f32 = jnp.float32
    mix = (jnp.dot(yc_ref[...], wo_ref[0:CONV_WIDTH, :], preferred_element_type=f32)
           + jnp.dot(ya_ref[...], wo_ref[CONV_WIDTH:, :], preferred_element_type=f32))
    h = x_ref[...] + mix
    r = _rms(h, np_ref[...]).astype(jnp.bfloat16)
    gate = jax.nn.sigmoid(jnp.dot(r, wg_ref[...], preferred_element_type=f32))
    pp = jnp.dot(p_ref[...].astype(jnp.bfloat16), wp_ref[...], preferred_element_type=f32)
    h = h + gate * pp
    o_ref[...] = _rms(h, fn_ref[...])


def _output(x2, yc2, ya2, p2, wo_bf, norm_ple, wg_bf, wp_bf, final_norm):
    t, d = x2.shape
    tm = OUT_TM
    row = lambda i: (i, 0)
    return pl.pallas_call(
        _out_kernel,
        grid=(t // tm,),
        in_specs=[
            pl.BlockSpec((tm, d), row),
            pl.BlockSpec((tm, CONV_WIDTH), row),
            pl.BlockSpec((tm, ATTN_WIDTH), row),
            pl.BlockSpec((tm, PLE_DIM), row),
            _const_spec(wo_bf.shape),
            _const_spec(norm_ple.shape),
            _const_spec(wg_bf.shape),
            _const_spec(wp_bf.shape),
            _const_spec(final_norm.shape),
        ],
        out_specs=pl.BlockSpec((tm, d), row),
        out_shape=jax.ShapeDtypeStruct((t, d), jnp.float32),
        compiler_params=pltpu.CompilerParams(
            dimension_semantics=("parallel",), vmem_limit_bytes=VMEM_LIMIT),
        name="out_ple_norm",
    )(x2, yc2, ya2, p2, wo_bf, norm_ple, wg_bf, wp_bf, final_norm)


def _rope_tables():
    half = ROT_DIM // 2
    inv_freq = ROPE_THETA ** (-jnp.arange(half, dtype=jnp.float32) / half)
    d = jnp.arange(LANES) % HEAD_DIM
    invf = jnp.where(d < ROT_DIM, inv_freq[d % half], 0.0).astype(jnp.float32)
    sgn = jnp.where(d < half, -1.0, jnp.where(d < ROT_DIM, 1.0, 0.0)).astype(jnp.float32)
    return invf[None, :], sgn[None, :]


def kernel(x, p, positions, norm_mix, w_in, conv_w, conv_b, lambda_q1, lambda_k1, lambda_q2,
           lambda_k2, subln_g, w_out, norm_ple, w_ple_gate, w_ple_proj, final_norm):
    b, s, d = x.shape
    depth = p.shape[0]
    assert depth == 1 and d == D_MODEL and w_in.shape[-1] == 8 * GROUP
    assert s % PROJ_TM == 0 and s % ATTN_TQ == 0 and (b * s) % OUT_TM == 0
    bf16 = jnp.bfloat16
    lam_init = 0.8 - 0.6 * math.exp(-0.3 * 0)

    w_bf = w_in[0].astype(bf16)
    wvt_bf = w_in[0][:, 6 * GROUP:7 * GROUP].T.astype(bf16)
    invf, sgn = _rope_tables()
    yconv, q, k, vt, gate = _projection(
        x, positions[:, :, None], norm_mix[0][None, :], w_bf, wvt_bf,
        conv_w[0], conv_b[0][None, :], invf, sgn)

    lam_params = jnp.stack([lambda_q1[0], lambda_k1[0], lambda_q2[0], lambda_k2[0]])
    yattn = _attention(q, k, vt, gate, lam_params, subln_g[0][None, :], lam_init)

    out = _output(
        x.reshape(b * s, d), yconv.reshape(b * s, CONV_WIDTH), yattn.reshape(b * s, ATTN_WIDTH),
        p[0].reshape(b * s, PLE_DIM), w_out[0].astype(bf16), norm_ple[0][None, :],
        w_ple_gate[0].astype(bf16), w_ple_proj[0].astype(bf16), final_norm[None, :])
    return out.reshape(b, s, d)
```

```python
import functools
import math

import jax
import jax.numpy as jnp
from jax import lax
from jax.experimental import pallas as pl
from jax.experimental.pallas import tpu as pltpu

D_MODEL = 1024
CHUNK = 64
PLE_DIM = 256
CONV_WIDTH = 512
CONV_K = 3
ATTN_WIDTH = 512
N_HEADS = 4
HEAD_DIM = 64
V_DIM = 2 * HEAD_DIM
ROT_DIM = HEAD_DIM // 4
ROPE_THETA = 500000.0
EPS = 1e-6
SUBLN_EPS = 1e-5
GROUP = 512
LANES = 128
SUBLANES = 8
NEG = -1e30

PROJ_TM = 512
ATTN_TQ = 256
ATTN_TK = 256
OUT_TM = 512
VMEM_LIMIT = 56 * 1024 * 1024

_NT = (((1,), (1,)), ((), ()))


def _const_spec(shape):
    return pl.BlockSpec(shape, lambda *_: (0,) * len(shape), pipeline_mode=pl.Buffered(1))


def _proj_kernel(x_ref, pos_ref, g_ref, w_ref, wvt_ref, cw_ref, cb_ref, invf_ref, sgn_ref,
                 yconv_ref, q_ref, k_ref, vt_ref, gate_ref, ubuf_ref, *, tm):
    si = pl.program_id(1)
    x = x_ref[0]
    ms = jnp.mean(x * x, axis=-1, keepdims=True)
    u = (x * lax.rsqrt(ms + EPS) * g_ref[...]).astype(jnp.bfloat16)

    def proj(c):
        return jnp.dot(u, w_ref[:, c * GROUP:(c + 1) * GROUP], preferred_element_type=jnp.float32)

    uc = proj(2) * proj(0)

    @pl.when(si == 0)
    def _():
        ubuf_ref[0:SUBLANES, :] = jnp.zeros((SUBLANES, GROUP), jnp.float32)

    @pl.when(si > 0)
    def _():
        ubuf_ref[0:SUBLANES, :] = ubuf_ref[tm:tm + SUBLANES, :]

    ubuf_ref[SUBLANES:tm + SUBLANES, :] = uc
    u1 = ubuf_ref[SUBLANES - 1:tm + SUBLANES - 1, :]
    u2 = ubuf_ref[SUBLANES - 2:tm + SUBLANES - 2, :]
    conv = cw_ref[0:1, :] * u2 + cw_ref[1:2, :] * u1 + cw_ref[2:3, :] * uc + cb_ref[...]
    cz = proj(3)
    yconv_ref[0] = (proj(1) * conv * (cz * jax.nn.sigmoid(cz))).astype(jnp.bfloat16)

    ang = pos_ref[0].astype(jnp.float32) * invf_ref[...]
    cos = jnp.cos(ang)
    sin = jnp.sin(ang) * sgn_ref[...]
    lane = lax.broadcasted_iota(jnp.int32, (1, LANES), 1)
    low = (lane % HEAD_DIM) < (ROT_DIM // 2)

    def rope(t, scale):
        outs = []
        for h in range(N_HEADS):
            th = t[:, h * LANES:(h + 1) * LANES]
            partner = jnp.where(low, pltpu.roll(th, LANES - ROT_DIM // 2, 1),
                                pltpu.roll(th, ROT_DIM // 2, 1))
            r = th * cos + partner * sin
            outs.append(r * scale if scale != 1.0 else r)
        return jnp.concatenate(outs, axis=1)

    q_ref[0] = rope(proj(4), HEAD_DIM ** -0.5 * math.log2(math.e)).astype(jnp.bfloat16)
    k_ref[0] = rope(proj(5), 1.0).astype(jnp.bfloat16)
    vt_ref[0] = lax.dot_general(wvt_ref[...], u, _NT,
                                preferred_element_type=jnp.float32).astype(jnp.bfloat16)
    az = proj(7)
    gate_ref[0] = (az * jax.nn.sigmoid(az)).astype(jnp.bfloat16)


def _projection(x, pos3, g, w_bf, wvt_bf, cw, cb, invf, sgn):
    b, s, d = x.shape
    tm = PROJ_TM
    tok = lambda bi, si: (bi, si, 0)
    out_tok = jax.ShapeDtypeStruct((b, s, GROUP), jnp.bfloat16)
    return pl.pallas_call(
        functools.partial(_proj_kernel, tm=tm),
        grid=(b, s // tm),
        in_specs=[
            pl.BlockSpec((1, tm, d), tok),
            pl.BlockSpec((1, tm, 1), tok),
            _const_spec((1, d)),
            _const_spec(w_bf.shape),
            _const_spec(wvt_bf.shape),
            _const_spec(cw.shape),
            _const_spec(cb.shape),
            _const_spec(invf.shape),
            _const_spec(sgn.shape),
        ],
        out_specs=[
            pl.BlockSpec((1, tm, GROUP), tok),
            pl.BlockSpec((1, tm, GROUP), tok),
            pl.BlockSpec((1, tm, GROUP), tok),
            pl.BlockSpec((1, GROUP, tm), lambda bi, si: (bi, 0, si)),
            pl.BlockSpec((1, tm, GROUP), tok),
        ],
        out_shape=[out_tok, out_tok, out_tok,
                   jax.ShapeDtypeStruct((b, GROUP, s), jnp.bfloat16), out_tok],
        scratch_shapes=[pltpu.VMEM((tm + SUBLANES, GROUP), jnp.float32)],
        compiler_params=pltpu.CompilerParams(
            dimension_semantics=("parallel", "arbitrary"), vmem_limit_bytes=VMEM_LIMIT),
        name="proj_conv_rope",
    )(x, pos3, g, w_bf, wvt_bf, cw, cb, invf, sgn)


def _attn_kernel(q_ref, k_ref, vt_ref, gate_ref, lam_ref, g_ref, y_ref, acc_ref,
                 *, tq, tk, lam_init):
    qi = pl.program_id(1)
    lp = lam_ref[...]
    lam = (jnp.exp(jnp.sum(lp[0:1] * lp[1:2], axis=1, keepdims=True))
           - jnp.exp(jnp.sum(lp[2:3] * lp[3:4], axis=1, keepdims=True)) + lam_init)
    lane = lax.broadcasted_iota(jnp.int32, (1, LANES), 1)
    kchunk = lax.broadcasted_iota(jnp.int32, (tk, tq), 0) // CHUNK
    qchunk = lax.broadcasted_iota(jnp.int32, (tk, tq), 1) // CHUNK
    diag_mask = kchunk <= qchunk

    heads = [slice(h * LANES, (h + 1) * LANES) for h in range(N_HEADS)]
    qz = []
    for hs in heads:
        qh = q_ref[0, :, hs]
        zero = jnp.zeros_like(qh)
        qz += [jnp.where(lane < HEAD_DIM, qh, zero), jnp.where(lane >= HEAD_DIM, qh, zero)]
    n_chain = 2 * N_HEADS

    def step(ki, carry, masked):
        off = pl.multiple_of(ki * tk, tk)
        sts = []
        for c in range(n_chain):
            kh = k_ref[0, pl.ds(off, tk), heads[c // 2]]
            sts.append(lax.dot_general(kh, qz[c], _NT, preferred_element_type=jnp.float32))
        new, ps, alphas = [], [], []
        for c in range(n_chain):
            m_old, l_old = carry[2 * c], carry[2 * c + 1]
            st = jnp.where(diag_mask, sts[c], NEG) if masked else sts[c]
            m_new = jnp.maximum(m_old, jnp.max(st, axis=0, keepdims=True))
            alpha = jnp.exp2(m_old - m_new)
            p = jnp.exp2(st - m_new)
            new += [m_new, alpha * l_old + jnp.sum(p, axis=0, keepdims=True)]
            ps.append(p.astype(jnp.bfloat16))
            alphas.append(alpha)
        for c in range(n_chain):
            vh = vt_ref[0, heads[c // 2], pl.ds(off, tk)]
            acc_ref[c] = alphas[c] * acc_ref[c] + jnp.dot(
                vh, ps[c], preferred_element_type=jnp.float32)
        return tuple(new)

    acc_ref[...] = jnp.zeros_like(acc_ref)
    m0 = jnp.full((1, tq), NEG, jnp.float32)
    l0 = jnp.zeros((1, tq), jnp.float32)
    carry = lax.fori_loop(0, qi, lambda ki, c: step(ki, c, False), (m0, l0) * n_chain)
    carry = step(qi, carry, True)

    for h, hs in enumerate(heads):
        l1, l2 = carry[4 * h + 1], carry[4 * h + 3]
        o = acc_ref[2 * h] * (1.0 / l1) - lam * (acc_ref[2 * h + 1] * (1.0 / l2))
        on = o * lax.rsqrt(jnp.mean(o * o, axis=0, keepdims=True) + SUBLN_EPS)
        y = on.T * (g_ref[...] * (1.0 - lam_init)) * gate_ref[0, :, hs].astype(jnp.float32)
        y_ref[0, :, hs] = y.astype(jnp.bfloat16)


def _attention(q, k, vt, gate, lam_params, subln_g, lam_init):
    b, s, w = q.shape
    tq, tk = ATTN_TQ, ATTN_TK
    assert tq == tk and tq % CHUNK == 0
    qtile = lambda bi, qi: (bi, qi, 0)
    return pl.pallas_call(
        functools.partial(_attn_kernel, tq=tq, tk=tk, lam_init=lam_init),
        grid=(b, s // tq),
        in_specs=[
            pl.BlockSpec((1, tq, w), qtile),
            pl.BlockSpec((1, s, w), lambda bi, qi: (bi, 0, 0)),
            pl.BlockSpec((1, w, s), lambda bi, qi: (bi, 0, 0)),
            pl.BlockSpec((1, tq, w), qtile),
            _const_spec(lam_params.shape),
            _const_spec(subln_g.shape),
        ],
        out_specs=pl.BlockSpec((1, tq, w), qtile),
        out_shape=jax.ShapeDtypeStruct((b, s, w), jnp.bfloat16),
        scratch_shapes=[pltpu.VMEM((2 * N_HEADS, V_DIM, tq), jnp.float32)],
        compiler_params=pltpu.CompilerParams(
            dimension_semantics=("parallel", "arbitrary"), vmem_limit_bytes=VMEM_LIMIT),
        name="diff_attention",
    )(q, k, vt, gate, lam_params, subln_g)


def _rms(h, g):
    return h * lax.rsqrt(jnp.mean(h * h, axis=-1, keepdims=True) + EPS) * g


def _out_kernel(x_ref, yc_ref, ya_ref, p_ref, wo_ref, np_ref, wg_ref, wp_ref, fn_ref, o_ref):
    f32 = jnp.float32
    mix = (jnp.dot(yc_ref[...], wo_ref[0:CONV_WIDTH, :], preferred_element_type=f32)
           + jnp.dot(ya_ref[...], wo_ref[CONV_WIDTH:, :], preferred_element_type=f32))
    h = x_ref[...] + mix
    r = _rms(h, np_ref[...]).astype(jnp.bfloat16)
    gate = jax.nn.sigmoid(jnp.dot(r, wg_ref[...], preferred_element_type=f32))
    pp = jnp.dot(p_ref[...].astype(jnp.bfloat16), wp_ref[...], preferred_element_type=f32)
    h = h + gate * pp
    o_ref[...] = _rms(h, fn_ref[...])


def _output(x2, yc2, ya2, p2, wo_bf, norm_ple, wg_bf, wp_bf, final_norm):
    t, d = x2.shape
    tm = OUT_TM
    row = lambda i: (i, 0)
    return pl.pallas_call(
        _out_kernel,
        grid=(t // tm,),
        in_specs=[
            pl.BlockSpec((tm, d), row),
            pl.BlockSpec((tm, CONV_WIDTH), row),
            pl.BlockSpec((tm, ATTN_WIDTH), row),
            pl.BlockSpec((tm, PLE_DIM), row),
            _const_spec(wo_bf.shape),
            _const_spec(norm_ple.shape),
            _const_spec(wg_bf.shape),
            _const_spec(wp_bf.shape),
            _const_spec(final_norm.shape),
        ],
        out_specs=pl.BlockSpec((tm, d), row),
        out_shape=jax.ShapeDtypeStruct((t, d), jnp.float32),
        compiler_params=pltpu.CompilerParams(
            dimension_semantics=("parallel",), vmem_limit_bytes=VMEM_LIMIT),
        name="out_ple_norm",
    )(x2, yc2, ya2, p2, wo_bf, norm_ple, wg_bf, wp_bf, final_norm)


def _rope_tables():
    half = ROT_DIM // 2
    inv_freq = ROPE_THETA ** (-jnp.arange(half, dtype=jnp.float32) / half)
    d = jnp.arange(LANES) % HEAD_DIM
    invf = jnp.where(d < ROT_DIM, inv_freq[d % half], 0.0).astype(jnp.float32)
    sgn = jnp.where(d < half, -1.0, jnp.where(d < ROT_DIM, 1.0, 0.0)).astype(jnp.float32)
    return invf[None, :], sgn[None, :]


def kernel(x, p, positions, norm_mix, w_in, conv_w, conv_b, lambda_q1, lambda_k1, lambda_q2,
           lambda_k2, subln_g, w_out, norm_ple, w_ple_gate, w_ple_proj, final_norm):
    b, s, d = x.shape
    depth = p.shape[0]
    assert depth == 1 and d == D_MODEL and w_in.shape[-1] == 8 * GROUP
    assert s % PROJ_TM == 0 and s % ATTN_TQ == 0 and (b * s) % OUT_TM == 0
    bf16 = jnp.bfloat16
    lam_init = 0.8 - 0.6 * math.exp(-0.3 * 0)

    w_bf = w_in[0].astype(bf16)
    wvt_bf = w_in[0][:, 6 * GROUP:7 * GROUP].T.astype(bf16)
    invf, sgn = _rope_tables()
    yconv, q, k, vt, gate = _projection(
        x, positions[:, :, None], norm_mix[0][None, :], w_bf, wvt_bf,
        conv_w[0], conv_b[0][None, :], invf, sgn)

    lam_params = jnp.stack([lambda_q1[0], lambda_k1[0], lambda_q2[0], lambda_k2[0]])
    yattn = _attention(q, k, vt, gate, lam_params, subln_g[0][None, :], lam_init)

    out = _output(
        x.reshape(b * s, d), yconv.reshape(b * s, CONV_WIDTH), yattn.reshape(b * s, ATTN_WIDTH),
        p[0].reshape(b * s, PLE_DIM), w_out[0].astype(bf16), norm_ple[0][None, :],
        w_ple_gate[0].astype(bf16), w_ple_proj[0].astype(bf16), final_norm[None, :])
    return out.reshape(b, s, d)
```

```python
import functools
import math

import jax
import jax.numpy as jnp
from jax import lax
from jax.experimental import pallas as pl
from jax.experimental.pallas import tpu as pltpu

D_MODEL = 1024
CHUNK = 64
PLE_DIM = 256
CONV_WIDTH = 512
CONV_K = 3
ATTN_WIDTH = 512
N_HEADS = 4
HEAD_DIM = 64
V_DIM = 2 * HEAD_DIM
ROT_DIM = HEAD_DIM // 4
ROPE_THETA = 500000.0
EPS = 1e-6
SUBLN_EPS = 1e-5
GROUP = 512
LANES = 128
SUBLANES = 8
NEG = -1e30

PROJ_TM = 512
ATTN_TQ = 512
ATTN_TK = 256
OUT_TM = 512
VMEM_LIMIT = 56 * 1024 * 1024

_NT = (((1,), (1,)), ((), ()))


def _const_spec(shape):
    return pl.BlockSpec(shape, lambda *_: (0,) * len(shape), pipeline_mode=pl.Buffered(1))


def _proj_kernel(x_ref, pos_ref, g_ref, w_ref, wvt_ref, cw_ref, cb_ref, invf_ref, sgn_ref,
                 yconv_ref, q_ref, k_ref, vt_ref, gate_ref, ubuf_ref, *, tm):
    si = pl.program_id(1)
    x = x_ref[0]
    ms = jnp.mean(x * x, axis=-1, keepdims=True)
    u = (x * lax.rsqrt(ms + EPS) * g_ref[...]).astype(jnp.bfloat16)

    def proj(c):
        return jnp.dot(u, w_ref[:, c * GROUP:(c + 1) * GROUP], preferred_element_type=jnp.float32)

    uc = proj(2) * proj(0)

    @pl.when(si == 0)
    def _():
        ubuf_ref[0:SUBLANES, :] = jnp.zeros((SUBLANES, GROUP), jnp.float32)

    @pl.when(si > 0)
    def _():
        ubuf_ref[0:SUBLANES, :] = ubuf_ref[tm:tm + SUBLANES, :]

    ubuf_ref[SUBLANES:tm + SUBLANES, :] = uc
    u1 = ubuf_ref[SUBLANES - 1:tm + SUBLANES - 1, :]
    u2 = ubuf_ref[SUBLANES - 2:tm + SUBLANES - 2, :]
    conv = cw_ref[0:1, :] * u2 + cw_ref[1:2, :] * u1 + cw_ref[2:3, :] * uc + cb_ref[...]
    cz = proj(3)
    yconv_ref[0] = (proj(1) * conv * (cz * jax.nn.sigmoid(cz))).astype(jnp.bfloat16)

    ang = pos_ref[0].astype(jnp.float32) * invf_ref[...]
    cos = jnp.cos(ang)
    sin = jnp.sin(ang) * sgn_ref[...]
    lane = lax.broadcasted_iota(jnp.int32, (1, LANES), 1)
    low = (lane % HEAD_DIM) < (ROT_DIM // 2)

    def rope(t, scale):
        outs = []
        for h in range(N_HEADS):
            th = t[:, h * LANES:(h + 1) * LANES]
            partner = jnp.where(low, pltpu.roll(th, LANES - ROT_DIM // 2, 1),
                                pltpu.roll(th, ROT_DIM // 2, 1))
            r = th * cos + partner * sin
            outs.append(r * scale if scale != 1.0 else r)
        return jnp.concatenate(outs, axis=1)

    q_ref[0] = rope(proj(4), HEAD_DIM ** -0.5 * math.log2(math.e)).astype(jnp.bfloat16)
    k_ref[0] = rope(proj(5), 1.0).astype(jnp.bfloat16)
    vt_ref[0] = lax.dot_general(wvt_ref[...], u, _NT,
                                preferred_element_type=jnp.float32).astype(jnp.bfloat16)
    az = proj(7)
    gate_ref[0] = (az * jax.nn.sigmoid(az)).astype(jnp.bfloat16)


def _projection(x, pos3, g, w_bf, wvt_bf, cw, cb, invf, sgn):
    b, s, d = x.shape
    tm = PROJ_TM
    tok = lambda bi, si: (bi, si, 0)
    out_tok = jax.ShapeDtypeStruct((b, s, GROUP), jnp.bfloat16)
    return pl.pallas_call(
        functools.partial(_proj_kernel, tm=tm),
        grid=(b, s // tm),
        in_specs=[
            pl.BlockSpec((1, tm, d), tok),
            pl.BlockSpec((1, tm, 1), tok),
            _const_spec((1, d)),
            _const_spec(w_bf.shape),
            _const_spec(wvt_bf.shape),
            _const_spec(cw.shape),
            _const_spec(cb.shape),
            _const_spec(invf.shape),
            _const_spec(sgn.shape),
        ],
        out_specs=[
            pl.BlockSpec((1, tm, GROUP), tok),
            pl.BlockSpec((1, tm, GROUP), tok),
            pl.BlockSpec((1, tm, GROUP), tok),
            pl.BlockSpec((1, GROUP, tm), lambda bi, si: (bi, 0, si)),
            pl.BlockSpec((1, tm, GROUP), tok),
        ],
        out_shape=[out_tok, out_tok, out_tok,
                   jax.ShapeDtypeStruct((b, GROUP, s), jnp.bfloat16), out_tok],
        scratch_shapes=[pltpu.VMEM((tm + SUBLANES, GROUP), jnp.float32)],
        compiler_params=pltpu.CompilerParams(
            dimension_semantics=("parallel", "arbitrary"), vmem_limit_bytes=VMEM_LIMIT),
        name="proj_conv_rope",
    )(x, pos3, g, w_bf, wvt_bf, cw, cb, invf, sgn)


def _attn_kernel(q_ref, k_ref, vt_ref, gate_ref, lam_ref, g_ref, y_ref,
                 s_ref, mt_ref, qz_ref, acc_ref, m_ref, l_ref, *, tq, tk, lam_init):
    qt = pl.program_id(1)
    n_chain = 2 * N_HEADS
    heads = [slice(h * LANES, (h + 1) * LANES) for h in range(N_HEADS)]
    lane = lax.broadcasted_iota(jnp.int32, (1, LANES), 1)
    kchunk = lax.broadcasted_iota(jnp.int32, (tk, tq), 0) // CHUNK
    qchunk = lax.broadcasted_iota(jnp.int32, (tk, tq), 1) // CHUNK

    def tail_mask(key_tile):
        return kchunk + key_tile * (tk // CHUNK) <= qchunk

    for h, hs in enumerate(heads):
        qh = q_ref[0, :, hs]
        zero = jnp.zeros_like(qh)
        qz_ref[2 * h] = jnp.where(lane < HEAD_DIM, qh, zero)
        qz_ref[2 * h + 1] = jnp.where(lane >= HEAD_DIM, qh, zero)
    acc_ref[...] = jnp.zeros_like(acc_ref)
    m_ref[...] = jnp.full_like(m_ref, NEG)
    l_ref[...] = jnp.zeros_like(l_ref)

    def qk(j, c, slot, mask):
        off = pl.multiple_of(j * tk, tk)
        kh = k_ref[0, pl.ds(off, tk), heads[c // 2]]
        st = lax.dot_general(kh, qz_ref[c], _NT, preferred_element_type=jnp.float32)
        if mask is not None:
            st = jnp.where(mask, st, NEG)
        s_ref[slot, c] = st
        mt_ref[slot, c] = jnp.max(st, axis=0, keepdims=True)

    def softmax_pv(j, c, slot):
        off = pl.multiple_of(j * tk, tk)
        m_old = m_ref[c]
        m_new = jnp.maximum(m_old, mt_ref[slot, c])
        alpha = jnp.exp2(m_old - m_new)
        p = jnp.exp2(s_ref[slot, c] - m_new)
        m_ref[c] = m_new
        l_ref[c] = alpha * l_ref[c] + jnp.sum(p, axis=0, keepdims=True)
        vh = vt_ref[0, heads[c // 2], pl.ds(off, tk)]
        acc_ref[c] = alpha * acc_ref[c] + jnp.dot(
            vh, p.astype(jnp.bfloat16), preferred_element_type=jnp.float32)

    def stage(j, slot, has_next=True, next_mask=None):
        for c in range(n_chain):
            if has_next:
                qk(j + 1, c, 1 - slot, next_mask)
            softmax_pv(j, c, slot)

    first_mask = jnp.logical_or(tail_mask(0), qt > 0)
    for c in range(n_chain):
        qk(0, c, 0, first_mask)

    def pair(jj, carry):
        stage(2 * jj, 0)
        stage(2 * jj + 1, 1)
        return carry

    lax.fori_loop(0, jnp.maximum(qt - 1, 0), pair, 0)

    @pl.when(qt > 0)
    def _():
        stage(2 * qt - 2, 0)
        stage(2 * qt - 1, 1, next_mask=tail_mask(0))

    stage(2 * qt, 0, next_mask=tail_mask(1))
    stage(2 * qt + 1, 1, has_next=False)

    lp = lam_ref[...]
    lam = (jnp.exp(jnp.sum(lp[0:1] * lp[1:2], axis=1, keepdims=True))
           - jnp.exp(jnp.sum(lp[2:3] * lp[3:4], axis=1, keepdims=True)) + lam_init)
    for h, hs in enumerate(heads):
        o = (acc_ref[2 * h] * (1.0 / l_ref[2 * h])
             - lam * (acc_ref[2 * h + 1] * (1.0 / l_ref[2 * h + 1])))
        on = o * lax.rsqrt(jnp.mean(o * o, axis=0, keepdims=True) + SUBLN_EPS)
        y = on.T * (g_ref[...] * (1.0 - lam_init)) * gate_ref[0, :, hs].astype(jnp.float32)
        y_ref[0, :, hs] = y.astype(jnp.bfloat16)


def _attention(q, k, vt, gate, lam_params, subln_g, lam_init):
    b, s, w = q.shape
    tq, tk = ATTN_TQ, ATTN_TK
    assert tq == 2 * tk and tk % CHUNK == 0
    n_chain = 2 * N_HEADS
    f32 = jnp.float32
    qtile = lambda bi, qi: (bi, qi, 0)
    return pl.pallas_call(
        functools.partial(_attn_kernel, tq=tq, tk=tk, lam_init=lam_init),
        grid=(b, s // tq),
        in_specs=[
            pl.BlockSpec((1, tq, w), qtile),
            pl.BlockSpec((1, s, w), lambda bi, qi: (bi, 0, 0)),
            pl.BlockSpec((1, w, s), lambda bi, qi: (bi, 0, 0)),
            pl.BlockSpec((1, tq, w), qtile),
            _const_spec(lam_params.shape),
            _const_spec(subln_g.shape),
        ],
        out_specs=pl.BlockSpec((1, tq, w), qtile),
        out_shape=jax.ShapeDtypeStruct((b, s, w), jnp.bfloat16),
        scratch_shapes=[
            pltpu.VMEM((2, n_chain, tk, tq), f32),
            pltpu.VMEM((2, n_chain, 1, tq), f32),
            pltpu.VMEM((n_chain, tq, LANES), jnp.bfloat16),
            pltpu.VMEM((n_chain, V_DIM, tq), f32),
            pltpu.VMEM((n_chain, 1, tq), f32),
            pltpu.VMEM((n_chain, 1, tq), f32),
        ],
        compiler_params=pltpu.CompilerParams(
            dimension_semantics=("parallel", "arbitrary"), vmem_limit_bytes=VMEM_LIMIT),
        name="diff_attention",
    )(q, k, vt, gate, lam_params, subln_g)


def _rms(h, g):
    return h * lax.rsqrt(jnp.mean(h * h, axis=-1, keepdims=True) + EPS) * g


def _out_kernel(x_ref, yc_ref, ya_ref, p_ref, wo_ref, np_ref, wg_ref, wp_ref, fn_ref, o_ref):
    f32 = jnp.float32
    mix = (jnp.dot(yc_ref[...], wo_ref[0:CONV_WIDTH, :], preferred_element_type=f32)
           + jnp.dot(ya_ref[...], wo_ref[CONV_WIDTH:, :], preferred_element_type=f32))
    h = x_ref[...] + mix
    r = _rms(h, np_ref[...]).astype(jnp.bfloat16)
    gate = jax.nn.sigmoid(jnp.dot(r, wg_ref[...], preferred_element_type=f32))
    pp = jnp.dot(p_ref[...].astype(jnp.bfloat16), wp_ref[...], preferred_element_type=f32)
    h = h + gate * pp
    o_ref[...] = _rms(h, fn_ref[...])


def _output(x2, yc2, ya2, p2, wo_bf, norm_ple, wg_bf, wp_bf, final_norm):
    t, d = x2.shape
    tm = OUT_TM
    row = lambda i: (i, 0)
    return pl.pallas_call(
        _out_kernel,
        grid=(t // tm,),
        in_specs=[
            pl.BlockSpec((tm, d), row),
            pl.BlockSpec((tm, CONV_WIDTH), row),
            pl.BlockSpec((tm, ATTN_WIDTH), row),
            pl.BlockSpec((tm, PLE_DIM), row),
            _const_spec(wo_bf.shape),
            _const_spec(norm_ple.shape),
            _const_spec(wg_bf.shape),
            _const_spec(wp_bf.shape),
            _const_spec(final_norm.shape),
        ],
        out_specs=pl.BlockSpec((tm, d), row),
        out_shape=jax.ShapeDtypeStruct((t, d), jnp.float32),
        compiler_params=pltpu.CompilerParams(
            dimension_semantics=("parallel",), vmem_limit_bytes=VMEM_LIMIT),
        name="out_ple_norm",
    )(x2, yc2, ya2, p2, wo_bf, norm_ple, wg_bf, wp_bf, final_norm)


def _rope_tables():
    half = ROT_DIM // 2
    inv_freq = ROPE_THETA ** (-jnp.arange(half, dtype=jnp.float32) / half)
    d = jnp.arange(LANES) % HEAD_DIM
    invf = jnp.where(d < ROT_DIM, inv_freq[d % half], 0.0).astype(jnp.float32)
    sgn = jnp.where(d < half, -1.0, jnp.where(d < ROT_DIM, 1.0, 0.0)).astype(jnp.float32)
    return invf[None, :], sgn[None, :]


def kernel(x, p, positions, norm_mix, w_in, conv_w, conv_b, lambda_q1, lambda_k1, lambda_q2,
           lambda_k2, subln_g, w_out, norm_ple, w_ple_gate, w_ple_proj, final_norm):
    b, s, d = x.shape
    depth = p.shape[0]
    assert depth == 1 and d == D_MODEL and w_in.shape[-1] == 8 * GROUP
    assert s % PROJ_TM == 0 and s % ATTN_TQ == 0 and (b * s) % OUT_TM == 0
    bf16 = jnp.bfloat16
    lam_init = 0.8 - 0.6 * math.exp(-0.3 * 0)

    w_bf = w_in[0].astype(bf16)
    wvt_bf = w_in[0][:, 6 * GROUP:7 * GROUP].T.astype(bf16)
    invf, sgn = _rope_tables()
    yconv, q, k, vt, gate = _projection(
        x, positions[:, :, None], norm_mix[0][None, :], w_bf, wvt_bf,
        conv_w[0], conv_b[0][None, :], invf, sgn)

    lam_params = jnp.stack([lambda_q1[0], lambda_k1[0], lambda_q2[0], lambda_k2[0]])
    yattn = _attention(q, k, vt, gate, lam_params, subln_g[0][None, :], lam_init)

    out = _output(
        x.reshape(b * s, d), yconv.reshape(b * s, CONV_WIDTH), yattn.reshape(b * s, ATTN_WIDTH),
        p[0].reshape(b * s, PLE_DIM), w_out[0].astype(bf16), norm_ple[0][None, :],
        w_ple_gate[0].astype(bf16), w_ple_proj[0].astype(bf16), final_norm[None, :])
    return out.reshape(b, s, d)
```

```python
import functools
import math

import jax
import jax.numpy as jnp
from jax import lax
from jax.experimental import pallas as pl
from jax.experimental.pallas import tpu as pltpu

D_MODEL = 1024
CHUNK = 64
PLE_DIM = 256
CONV_WIDTH = 512
CONV_K = 3
ATTN_WIDTH = 512
N_HEADS = 4
HEAD_DIM = 64
V_DIM = 2 * HEAD_DIM
V_EXT = V_DIM + 16
ROT_DIM = HEAD_DIM // 4
ROPE_THETA = 500000.0
EPS = 1e-6
SUBLN_EPS = 1e-5
GROUP = 512
LANES = 128
SUBLANES = 8
NEG = -1e30

PROJ_TM = 512
ATTN_TQ = 512
ATTN_TK = 256
OUT_TM = 512
VMEM_LIMIT = 56 * 1024 * 1024

_NT = (((1,), (1,)), ((), ()))


def _const_spec(shape):
    return pl.BlockSpec(shape, lambda *_: (0,) * len(shape), pipeline_mode=pl.Buffered(1))


def _proj_kernel(x_ref, pos_ref, g_ref, w_ref, wvt_ref, cw_ref, cb_ref, invf_ref,
                 yconv_ref, q_ref, k_ref, vt_ref, gate_ref, ubuf_ref, *, tm):
    si = pl.program_id(1)
    x = x_ref[0]
    ms = jnp.mean(x * x, axis=-1, keepdims=True)
    u = (x * lax.rsqrt(ms + EPS) * g_ref[...]).astype(jnp.bfloat16)

    def proj(c):
        return jnp.dot(u, w_ref[:, c * GROUP:(c + 1) * GROUP], preferred_element_type=jnp.float32)

    uc = proj(2) * proj(0)

    @pl.when(si == 0)
    def _():
        ubuf_ref[...] = jnp.zeros_like(ubuf_ref)

    prev = ubuf_ref[...]
    ubuf_ref[...] = uc[tm - SUBLANES:, :]
    row = lax.broadcasted_iota(jnp.int32, (SUBLANES, 1), 0)

    def shifted(k):
        r = pltpu.roll(uc, k, 0)
        head = jnp.where(row < k, pltpu.roll(prev, k, 0), r[0:SUBLANES, :])
        return jnp.concatenate([head, r[SUBLANES:, :]], axis=0)

    conv = (cw_ref[0:1, :] * shifted(2) + cw_ref[1:2, :] * shifted(1)
            + cw_ref[2:3, :] * uc + cb_ref[...])
    cz = proj(3)
    yconv_ref[0] = (proj(1) * conv * (cz * jax.nn.sigmoid(cz))).astype(jnp.bfloat16)

    ang = invf_ref[...] * pos_ref[0].astype(jnp.float32)
    c8, s8 = jnp.cos(ang), jnp.sin(ang)
    fill = HEAD_DIM - ROT_DIM
    one, zero = jnp.ones((fill, tm), jnp.float32), jnp.zeros((fill, tm), jnp.float32)
    cos = jnp.concatenate([c8, c8, one] * 2, axis=0).T
    sin = jnp.concatenate([-s8, s8, zero] * 2, axis=0).T
    lane = lax.broadcasted_iota(jnp.int32, (1, LANES), 1)
    low = (lane % HEAD_DIM) < (ROT_DIM // 2)

    def rope(t, scale):
        outs = []
        for h in range(N_HEADS):
            th = t[:, h * LANES:(h + 1) * LANES]
            partner = jnp.where(low, pltpu.roll(th, LANES - ROT_DIM // 2, 1),
                                pltpu.roll(th, ROT_DIM // 2, 1))
            r = th * cos + partner * sin
            outs.append(r * scale if scale != 1.0 else r)
        return jnp.concatenate(outs, axis=1)

    q_ref[0] = rope(proj(4), HEAD_DIM ** -0.5 * math.log2(math.e)).astype(jnp.bfloat16)
    k_ref[0] = rope(proj(5), 1.0).astype(jnp.bfloat16)
    vt = lax.dot_general(wvt_ref[...], u, _NT, preferred_element_type=jnp.float32).astype(jnp.bfloat16)
    ones = jnp.ones((V_EXT - V_DIM, tm), jnp.bfloat16)
    for h in range(N_HEADS):
        vt_ref[0, h * V_EXT:h * V_EXT + V_DIM, :] = vt[h * V_DIM:(h + 1) * V_DIM, :]
        vt_ref[0, h * V_EXT + V_DIM:(h + 1) * V_EXT, :] = ones
    az = proj(7)
    gate_ref[0] = (az * jax.nn.sigmoid(az)).astype(jnp.bfloat16)


def _projection(x, pos3, g, w_bf, wvt_bf, cw, cb, invf):
    b, s, d = x.shape
    tm = PROJ_TM
    tok = lambda bi, si: (bi, si, 0)
    out_tok = jax.ShapeDtypeStruct((b, s, GROUP), jnp.bfloat16)
    vt_rows = N_HEADS * V_EXT
    return pl.pallas_call(
        functools.partial(_proj_kernel, tm=tm),
        grid=(b, s // tm),
        in_specs=[
            pl.BlockSpec((1, tm, d), tok),
            pl.BlockSpec((1, 1, tm), lambda bi, si: (bi, 0, si)),
            _const_spec((1, d)),
            _const_spec(w_bf.shape),
            _const_spec(wvt_bf.shape),
            _const_spec(cw.shape),
            _const_spec(cb.shape),
            _const_spec(invf.shape),
        ],
        out_specs=[
            pl.BlockSpec((1, tm, GROUP), tok),
            pl.BlockSpec((1, tm, GROUP), tok),
            pl.BlockSpec((1, tm, GROUP), tok),
            pl.BlockSpec((1, vt_rows, tm), lambda bi, si: (bi, 0, si)),
            pl.BlockSpec((1, tm, GROUP), tok),
        ],
        out_shape=[out_tok, out_tok, out_tok,
                   jax.ShapeDtypeStruct((b, vt_rows, s), jnp.bfloat16), out_tok],
        scratch_shapes=[pltpu.VMEM((SUBLANES, GROUP), jnp.float32)],
        compiler_params=pltpu.CompilerParams(
            dimension_semantics=("parallel", "arbitrary"), vmem_limit_bytes=VMEM_LIMIT),
        name="proj_conv_rope",
    )(x, pos3, g, w_bf, wvt_bf, cw, cb, invf)


def _attn_kernel(q_ref, k_ref, vt_ref, gate_ref, lam_ref, g_ref, y_ref,
                 s_ref, mt_ref, qz_ref, acc_ref, m_ref, *, tq, tk, lam_init):
    qt = pl.program_id(1)
    n_chain = 2 * N_HEADS
    heads = [slice(h * LANES, (h + 1) * LANES) for h in range(N_HEADS)]
    lane = lax.broadcasted_iota(jnp.int32, (1, LANES), 1)
    kchunk = lax.broadcasted_iota(jnp.int32, (tk, tq), 0) // CHUNK
    qchunk = lax.broadcasted_iota(jnp.int32, (tk, tq), 1) // CHUNK

    def tail_mask(key_tile):
        return kchunk + key_tile * (tk // CHUNK) <= qchunk

    for h, hs in enumerate(heads):
        qh = q_ref[0, :, hs]
        zero = jnp.zeros_like(qh)
        qz_ref[2 * h] = jnp.where(lane < HEAD_DIM, qh, zero)
        qz_ref[2 * h + 1] = jnp.where(lane >= HEAD_DIM, qh, zero)
    acc_ref[...] = jnp.zeros_like(acc_ref)
    m_ref[...] = jnp.full_like(m_ref, NEG)

    def qk(j, c, slot, mask):
        off = pl.multiple_of(j * tk, tk)
        kh = k_ref[0, pl.ds(off, tk), heads[c // 2]]
        st = lax.dot_general(kh, qz_ref[c], _NT, preferred_element_type=jnp.float32)
        if mask is not None:
            st = jnp.where(mask, st, NEG)
        s_ref[slot, c] = st
        mt_ref[slot, c] = jnp.max(st, axis=0, keepdims=True)

    def softmax_pv(j, c, slot):
        off = pl.multiple_of(j * tk, tk)
        m_old = m_ref[c]
        m_new = jnp.maximum(m_old, mt_ref[slot, c])
        alpha = jnp.exp2(m_old - m_new)
        p = jnp.exp2(s_ref[slot, c] - m_new)
        m_ref[c] = m_new
        h = c // 2
        vh = vt_ref[0, h * V_EXT:(h + 1) * V_EXT, pl.ds(off, tk)]
        acc_ref[c] = alpha * acc_ref[c] + jnp.dot(
            vh, p.astype(jnp.bfloat16), preferred_element_type=jnp.float32)

    def stage(j, slot, has_next=True, next_mask=None):
        for c in range(n_chain):
            if has_next:
                qk(j + 1, c, 1 - slot, next_mask)
            softmax_pv(j, c, slot)

    first_mask = jnp.logical_or(tail_mask(0), qt > 0)
    for c in range(n_chain):
        qk(0, c, 0, first_mask)

    def pair(jj, carry):
        stage(2 * jj, 0)
        stage(2 * jj + 1, 1)
        return carry

    lax.fori_loop(0, jnp.maximum(qt - 1, 0), pair, 0)

    @pl.when(qt > 0)
    def _():
        stage(2 * qt - 2, 0)
        stage(2 * qt - 1, 1, next_mask=tail_mask(0))

    stage(2 * qt, 0, next_mask=tail_mask(1))
    stage(2 * qt + 1, 1, has_next=False)

    lp = lam_ref[...]
    lam = (jnp.exp(jnp.sum(lp[0:1] * lp[1:2], axis=1, keepdims=True))
           - jnp.exp(jnp.sum(lp[2:3] * lp[3:4], axis=1, keepdims=True)) + lam_init)
    for h, hs in enumerate(heads):
        a1, a2 = acc_ref[2 * h], acc_ref[2 * h + 1]
        o = (a1[:V_DIM] * (1.0 / a1[V_DIM:V_DIM + 1])
             - lam * (a2[:V_DIM] * (1.0 / a2[V_DIM:V_DIM + 1])))
        on = o * lax.rsqrt(jnp.mean(o * o, axis=0, keepdims=True) + SUBLN_EPS)
        y = on.T * (g_ref[...] * (1.0 - lam_init)) * gate_ref[0, :, hs].astype(jnp.float32)
        y_ref[0, :, hs] = y.astype(jnp.bfloat16)


def _attention(q, k, vt, gate, lam_params, subln_g, lam_init):
    b, s, w = q.shape
    tq, tk = ATTN_TQ, ATTN_TK
    assert tq == 2 * tk and tk % CHUNK == 0
    n_chain = 2 * N_HEADS
    f32 = jnp.float32
    qtile = lambda bi, qi: (bi, qi, 0)
    return pl.pallas_call(
        functools.partial(_attn_kernel, tq=tq, tk=tk, lam_init=lam_init),
        grid=(b, s // tq),
        in_specs=[
            pl.BlockSpec((1, tq, w), qtile),
            pl.BlockSpec((1, s, w), lambda bi, qi: (bi, 0, 0)),
            pl.BlockSpec((1, vt.shape[1], s), lambda bi, qi: (bi, 0, 0)),
            pl.BlockSpec((1, tq, w), qtile),
            _const_spec(lam_params.shape),
            _const_spec(subln_g.shape),
        ],
        out_specs=pl.BlockSpec((1, tq, w), qtile),
        out_shape=jax.ShapeDtypeStruct((b, s, w), jnp.bfloat16),
        scratch_shapes=[
            pltpu.VMEM((2, n_chain, tk, tq), f32),
            pltpu.VMEM((2, n_chain, 1, tq), f32),
            pltpu.VMEM((n_chain, tq, LANES), jnp.bfloat16),
            pltpu.VMEM((n_chain, V_EXT, tq), f32),
            pltpu.VMEM((n_chain, 1, tq), f32),
        ],
        compiler_params=pltpu.CompilerParams(
            dimension_semantics=("parallel", "arbitrary"), vmem_limit_bytes=VMEM_LIMIT),
        name="diff_attention",
    )(q, k, vt, gate, lam_params, subln_g)


def _rms(h, g):
    return h * lax.rsqrt(jnp.mean(h * h, axis=-1, keepdims=True) + EPS) * g


def _out_kernel(x_ref, yc_ref, ya_ref, p_ref, wo_ref, np_ref, wg_ref, wp_ref, fn_ref, o_ref):
    f32 = jnp.float32
    mix = (jnp.dot(yc_ref[...], wo_ref[0:CONV_WIDTH, :], preferred_element_type=f32)
           + jnp.dot(ya_ref[...], wo_ref[CONV_WIDTH:, :], preferred_element_type=f32))
    h = x_ref[...] + mix
    r = _rms(h, np_ref[...]).astype(jnp.bfloat16)
    gate = jax.nn.sigmoid(jnp.dot(r, wg_ref[...], preferred_element_type=f32))
    pp = jnp.dot(p_ref[...].astype(jnp.bfloat16), wp_ref[...], preferred_element_type=f32)
    h = h + gate * pp
    o_ref[...] = _rms(h, fn_ref[...])


def _output(x2, yc2, ya2, p2, wo_bf, norm_ple, wg_bf, wp_bf, final_norm):
    t, d = x2.shape
    tm = OUT_TM
    row = lambda i: (i, 0)
    return pl.pallas_call(
        _out_kernel,
        grid=(t // tm,),
        in_specs=[
            pl.BlockSpec((tm, d), row),
            pl.BlockSpec((tm, CONV_WIDTH), row),
            pl.BlockSpec((tm, ATTN_WIDTH), row),
            pl.BlockSpec((tm, PLE_DIM), row),
            _const_spec(wo_bf.shape),
            _const_spec(norm_ple.shape),
            _const_spec(wg_bf.shape),
            _const_spec(wp_bf.shape),
            _const_spec(final_norm.shape),
        ],
        out_specs=pl.BlockSpec((tm, d), row),
        out_shape=jax.ShapeDtypeStruct((t, d), jnp.float32),
        compiler_params=pltpu.CompilerParams(
            dimension_semantics=("parallel",), vmem_limit_bytes=VMEM_LIMIT),
        name="out_ple_norm",
    )(x2, yc2, ya2, p2, wo_bf, norm_ple, wg_bf, wp_bf, final_norm)


def kernel(x, p, positions, norm_mix, w_in, conv_w, conv_b, lambda_q1, lambda_k1, lambda_q2,
           lambda_k2, subln_g, w_out, norm_ple, w_ple_gate, w_ple_proj, final_norm):
    b, s, d = x.shape
    depth = p.shape[0]
    assert depth == 1 and d == D_MODEL and w_in.shape[-1] == 8 * GROUP
    assert s % PROJ_TM == 0 and s % ATTN_TQ == 0 and (b * s) % OUT_TM == 0
    bf16 = jnp.bfloat16
    lam_init = 0.8 - 0.6 * math.exp(-0.3 * 0)

    w_bf = w_in[0].astype(bf16)
    wvt_bf = w_in[0][:, 6 * GROUP:7 * GROUP].T.astype(bf16)
    half = ROT_DIM // 2
    invf = (ROPE_THETA ** (-jnp.arange(half, dtype=jnp.float32) / half))[:, None]
    yconv, q, k, vt, gate = _projection(
        x, positions[:, None, :], norm_mix[0][None, :], w_bf, wvt_bf,
        conv_w[0], conv_b[0][None, :], invf)

    lam_params = jnp.stack([lambda_q1[0], lambda_k1[0], lambda_q2[0], lambda_k2[0]])
    yattn = _attention(q, k, vt, gate, lam_params, subln_g[0][None, :], lam_init)

    out = _output(
        x.reshape(b * s, d), yconv.reshape(b * s, CONV_WIDTH), yattn.reshape(b * s, ATTN_WIDTH),
        p[0].reshape(b * s, PLE_DIM), w_out[0].astype(bf16), norm_ple[0][None, :],
        w_ple_gate[0].astype(bf16), w_ple_proj[0].astype(bf16), final_norm[None, :])
    return out.reshape(b, s, d)
```

```python
import functools
import math

import jax
import jax.numpy as jnp
from jax import lax
from jax.experimental import pallas as pl
from jax.experimental.pallas import tpu as pltpu

D_MODEL = 1024
CHUNK = 64
PLE_DIM = 256
CONV_WIDTH = 512
CONV_K = 3
ATTN_WIDTH = 512
N_HEADS = 4
HEAD_DIM = 64
V_DIM = 2 * HEAD_DIM
V_EXT = V_DIM + 16
ROT_DIM = HEAD_DIM // 4
ROPE_THETA = 500000.0
EPS = 1e-6
SUBLN_EPS = 1e-5
GROUP = 512
LANES = 128
SUBLANES = 8
NEG = -1e30

PROJ_TM = 512
PROJ_SUB = 256
ATTN_TQ = 512
ATTN_TK = 256
OUT_TM = 1024
OUT_SUB = 256
VMEM_LIMIT = 56 * 1024 * 1024

_NT = (((1,), (1,)), ((), ()))


def _const_spec(shape):
    return pl.BlockSpec(shape, lambda *_: (0,) * len(shape), pipeline_mode=pl.Buffered(1))


def _proj_kernel(x_ref, pos_ref, g_ref, w_ref, wvt_ref, cw_ref, cb_ref, invf_ref,
                 yconv_ref, q_ref, k_ref, vt_ref, gate_ref, ubuf_ref, *, tm, sub):
    si = pl.program_id(1)
    bf16 = jnp.bfloat16

    @pl.when(si == 0)
    def _():
        ubuf_ref[...] = jnp.zeros_like(ubuf_ref)

    prev = ubuf_ref[...]
    row = lax.broadcasted_iota(jnp.int32, (SUBLANES, 1), 0)
    lane = lax.broadcasted_iota(jnp.int32, (1, LANES), 1)
    low = (lane % HEAD_DIM) < (ROT_DIM // 2)
    fill = HEAD_DIM - ROT_DIM
    one, zero = jnp.ones((fill, sub), jnp.float32), jnp.zeros((fill, sub), jnp.float32)
    ones_rows = jnp.ones((V_EXT - V_DIM, sub), bf16)

    for r0 in range(0, tm, sub):
        rows = slice(r0, r0 + sub)
        x = x_ref[0, rows, :]
        ms = jnp.mean(x * x, axis=-1, keepdims=True)
        u = (x * lax.rsqrt(ms + EPS) * g_ref[...]).astype(bf16)

        def proj(c):
            return jnp.dot(u, w_ref[:, c * GROUP:(c + 1) * GROUP], preferred_element_type=jnp.float32)

        uc = proj(2) * proj(0)

        def shifted(k):
            r = pltpu.roll(uc, k, 0)
            head = jnp.where(row < k, pltpu.roll(prev, k, 0), r[0:SUBLANES, :])
            return jnp.concatenate([head, r[SUBLANES:, :]], axis=0)

        conv = (cw_ref[0:1, :] * shifted(2) + cw_ref[1:2, :] * shifted(1)
                + cw_ref[2:3, :] * uc + cb_ref[...])
        prev = uc[sub - SUBLANES:, :]
        cz = proj(3)
        yconv_ref[0, rows, :] = (proj(1) * conv * (cz * jax.nn.sigmoid(cz))).astype(bf16)

        ang = invf_ref[...] * pos_ref[0, :, rows].astype(jnp.float32)
        c8, s8 = jnp.cos(ang), jnp.sin(ang)
        cos = jnp.concatenate([c8, c8, one] * 2, axis=0).T
        sin = jnp.concatenate([-s8, s8, zero] * 2, axis=0).T

        def rope(t, scale):
            outs = []
            for h in range(N_HEADS):
                th = t[:, h * LANES:(h + 1) * LANES]
                partner = jnp.where(low, pltpu.roll(th, LANES - ROT_DIM // 2, 1),
                                    pltpu.roll(th, ROT_DIM // 2, 1))
                r = th * cos + partner * sin
                outs.append(r * scale if scale != 1.0 else r)
            return jnp.concatenate(outs, axis=1)

        q_ref[0, rows, :] = rope(proj(4), HEAD_DIM ** -0.5 * math.log2(math.e)).astype(bf16)
        k_ref[0, rows, :] = rope(proj(5), 1.0).astype(bf16)
        az = proj(7)
        gate_ref[0, rows, :] = (az * jax.nn.sigmoid(az)).astype(bf16)
        vt = lax.dot_general(wvt_ref[...], u, _NT, preferred_element_type=jnp.float32).astype(bf16)
        for h in range(N_HEADS):
            vt_ref[0, h * V_EXT:h * V_EXT + V_DIM, rows] = vt[h * V_DIM:(h + 1) * V_DIM, :]
            vt_ref[0, h * V_EXT + V_DIM:(h + 1) * V_EXT, rows] = ones_rows

    ubuf_ref[...] = prev


def _projection(x, pos3, g, w_bf, wvt_bf, cw, cb, invf):
    b, s, d = x.shape
    tm = PROJ_TM
    tok = lambda bi, si: (bi, si, 0)
    out_tok = jax.ShapeDtypeStruct((b, s, GROUP), jnp.bfloat16)
    vt_rows = N_HEADS * V_EXT
    return pl.pallas_call(
        functools.partial(_proj_kernel, tm=tm, sub=PROJ_SUB),
        grid=(b, s // tm),
        in_specs=[
            pl.BlockSpec((1, tm, d), tok),
            pl.BlockSpec((1, 1, tm), lambda bi, si: (bi, 0, si)),
            _const_spec((1, d)),
            _const_spec(w_bf.shape),
            _const_spec(wvt_bf.shape),
            _const_spec(cw.shape),
            _const_spec(cb.shape),
            _const_spec(invf.shape),
        ],
        out_specs=[
            pl.BlockSpec((1, tm, GROUP), tok),
            pl.BlockSpec((1, tm, GROUP), tok),
            pl.BlockSpec((1, tm, GROUP), tok),
            pl.BlockSpec((1, vt_rows, tm), lambda bi, si: (bi, 0, si)),
            pl.BlockSpec((1, tm, GROUP), tok),
        ],
        out_shape=[out_tok, out_tok, out_tok,
                   jax.ShapeDtypeStruct((b, vt_rows, s), jnp.bfloat16), out_tok],
        scratch_shapes=[pltpu.VMEM((SUBLANES, GROUP), jnp.float32)],
        compiler_params=pltpu.CompilerParams(
            dimension_semantics=("parallel", "arbitrary"), vmem_limit_bytes=VMEM_LIMIT),
        name="proj_conv_rope",
    )(x, pos3, g, w_bf, wvt_bf, cw, cb, invf)


def _attn_kernel(q_ref, k_ref, vt_ref, gate_ref, lam_ref, g_ref, y_ref,
                 s_ref, mt_ref, qz_ref, acc_ref, m_ref, *, tq, tk, lam_init):
    qt = pl.program_id(1)
    n_chain = 2 * N_HEADS
    heads = [slice(h * LANES, (h + 1) * LANES) for h in range(N_HEADS)]
    lane = lax.broadcasted_iota(jnp.int32, (1, LANES), 1)
    every = slice(0, tq)
    upper = slice(tq // 2, tq)

    def tail_mask(key_tile, cols=every):
        n = cols.stop - cols.start
        kchunk = lax.broadcasted_iota(jnp.int32, (tk, n), 0) // CHUNK + key_tile * (tk // CHUNK)
        qchunk = (lax.broadcasted_iota(jnp.int32, (tk, n), 1) + cols.start) // CHUNK
        return kchunk <= qchunk

    for h, hs in enumerate(heads):
        qh = q_ref[0, :, hs]
        zero = jnp.zeros_like(qh)
        qz_ref[2 * h] = jnp.where(lane < HEAD_DIM, qh, zero)
        qz_ref[2 * h + 1] = jnp.where(lane >= HEAD_DIM, qh, zero)
    acc_ref[...] = jnp.zeros_like(acc_ref)
    m_ref[...] = jnp.full_like(m_ref, NEG)

    def qk(j, c, slot, mask, cols=every):
        off = pl.multiple_of(j * tk, tk)
        kh = k_ref[0, pl.ds(off, tk), heads[c // 2]]
        st = lax.dot_general(kh, qz_ref[c, cols, :], _NT, preferred_element_type=jnp.float32)
        if mask is not None:
            st = jnp.where(mask, st, NEG)
        s_ref[slot, c, :, cols] = st
        mt_ref[slot, c, :, cols] = jnp.max(st, axis=0, keepdims=True)

    def softmax_pv(j, c, slot, cols=every):
        off = pl.multiple_of(j * tk, tk)
        m_old = m_ref[c, :, cols]
        m_new = jnp.maximum(m_old, mt_ref[slot, c, :, cols])
        alpha = jnp.exp2(m_old - m_new)
        p = jnp.exp2(s_ref[slot, c, :, cols] - m_new)
        m_ref[c, :, cols] = m_new
        h = c // 2
        vh = vt_ref[0, h * V_EXT:(h + 1) * V_EXT, pl.ds(off, tk)]
        acc_ref[c, :, cols] = alpha * acc_ref[c, :, cols] + jnp.dot(
            vh, p.astype(jnp.bfloat16), preferred_element_type=jnp.float32)

    lp = lam_ref[...]
    lam = (jnp.exp(jnp.sum(lp[0:1] * lp[1:2], axis=1, keepdims=True))
           - jnp.exp(jnp.sum(lp[2:3] * lp[3:4], axis=1, keepdims=True)) + lam_init)

    def finalize(h):
        hs = heads[h]
        a1, a2 = acc_ref[2 * h], acc_ref[2 * h + 1]
        o = (a1[:V_DIM] * (1.0 / a1[V_DIM:V_DIM + 1])
             - a2[:V_DIM] * (lam / a2[V_DIM:V_DIM + 1]))
        on = o * lax.rsqrt(jnp.mean(o * o, axis=0, keepdims=True) + SUBLN_EPS)
        y = on.T * (g_ref[...] * (1.0 - lam_init)) * gate_ref[0, :, hs].astype(jnp.float32)
        y_ref[0, :, hs] = y.astype(jnp.bfloat16)

    def stage(j, slot, has_next=True, next_mask=None, cols=every, next_cols=every, last=False):
        for c in range(n_chain):
            if has_next:
                qk(j + 1, c, 1 - slot, next_mask, next_cols)
            softmax_pv(j, c, slot, cols)
            if last and c % 2 == 1:
                finalize(c // 2)

    first_mask = jnp.logical_or(tail_mask(0), qt > 0)
    for c in range(n_chain):
        qk(0, c, 0, first_mask)

    def pair(jj, carry):
        stage(2 * jj, 0)
        stage(2 * jj + 1, 1)
        return carry

    lax.fori_loop(0, jnp.maximum(qt - 1, 0), pair, 0)

    @pl.when(qt > 0)
    def _():
        stage(2 * qt - 2, 0)
        stage(2 * qt - 1, 1, next_mask=tail_mask(0))

    stage(2 * qt, 0, next_mask=tail_mask(1, upper), next_cols=upper)
    stage(2 * qt + 1, 1, has_next=False, cols=upper, last=True)


def _attention(q, k, vt, gate, lam_params, subln_g, lam_init):
    b, s, w = q.shape
    tq, tk = ATTN_TQ, ATTN_TK
    assert tq == 2 * tk and tk % CHUNK == 0
    n_chain = 2 * N_HEADS
    f32 = jnp.float32
    qtile = lambda bi, qi: (bi, qi, 0)
    return pl.pallas_call(
        functools.partial(_attn_kernel, tq=tq, tk=tk, lam_init=lam_init),
        grid=(b, s // tq),
        in_specs=[
            pl.BlockSpec((1, tq, w), qtile),
            pl.BlockSpec((1, s, w), lambda bi, qi: (bi, 0, 0)),
            pl.BlockSpec((1, vt.shape[1], s), lambda bi, qi: (bi, 0, 0)),
            pl.BlockSpec((1, tq, w), qtile),
            _const_spec(lam_params.shape),
            _const_spec(subln_g.shape),
        ],
        out_specs=pl.BlockSpec((1, tq, w), qtile),
        out_shape=jax.ShapeDtypeStruct((b, s, w), jnp.bfloat16),
        scratch_shapes=[
            pltpu.VMEM((2, n_chain, tk, tq), f32),
            pltpu.VMEM((2, n_chain, 1, tq), f32),
            pltpu.VMEM((n_chain, tq, LANES), jnp.bfloat16),
            pltpu.VMEM((n_chain, V_EXT, tq), f32),
            pltpu.VMEM((n_chain, 1, tq), f32),
        ],
        compiler_params=pltpu.CompilerParams(
            dimension_semantics=("parallel", "arbitrary"), vmem_limit_bytes=VMEM_LIMIT),
        name="diff_attention",
    )(q, k, vt, gate, lam_params, subln_g)


def _rms(h, g):
    return h * lax.rsqrt(jnp.mean(h * h, axis=-1, keepdims=True) + EPS) * g


def _out_kernel(x_ref, yc_ref, ya_ref, p_ref, wo_ref, np_ref, wg_ref, wp_ref, fn_ref, o_ref,
                *, tm, sub):
    f32 = jnp.float32
    for r0 in range(0, tm, sub):
        rows = slice(r0, r0 + sub)
        mix = (jnp.dot(yc_ref[rows, :], wo_ref[0:CONV_WIDTH, :], preferred_element_type=f32)
               + jnp.dot(ya_ref[rows, :], wo_ref[CONV_WIDTH:, :], preferred_element_type=f32))
        h = x_ref[rows, :] + mix
        r = _rms(h, np_ref[...]).astype(jnp.bfloat16)
        gate = jax.nn.sigmoid(jnp.dot(r, wg_ref[...], preferred_element_type=f32))
        pp = jnp.dot(p_ref[rows, :].astype(jnp.bfloat16), wp_ref[...], preferred_element_type=f32)
        h = h + gate * pp
        o_ref[rows, :] = _rms(h, fn_ref[...])


def _output(x2, yc2, ya2, p2, wo_bf, norm_ple, wg_bf, wp_bf, final_norm):
    t, d = x2.shape
    tm = OUT_TM
    row = lambda i: (i, 0)
    return pl.pallas_call(
        functools.partial(_out_kernel, tm=tm, sub=OUT_SUB),
        grid=(t // tm,),
        in_specs=[
            pl.BlockSpec((tm, d), row),
            pl.BlockSpec((tm, CONV_WIDTH), row),
            pl.BlockSpec((tm, ATTN_WIDTH), row),
            pl.BlockSpec((tm, PLE_DIM), row),
            _const_spec(wo_bf.shape),
            _const_spec(norm_ple.shape),
            _const_spec(wg_bf.shape),
            _const_spec(wp_bf.shape),
            _const_spec(final_norm.shape),
        ],
        out_specs=pl.BlockSpec((tm, d), row),
        out_shape=jax.ShapeDtypeStruct((t, d), jnp.float32),
        compiler_params=pltpu.CompilerParams(
            dimension_semantics=("parallel",), vmem_limit_bytes=VMEM_LIMIT),
        name="out_ple_norm",
    )(x2, yc2, ya2, p2, wo_bf, norm_ple, wg_bf, wp_bf, final_norm)


def kernel(x, p, positions, norm_mix, w_in, conv_w, conv_b, lambda_q1, lambda_k1, lambda_q2,
           lambda_k2, subln_g, w_out, norm_ple, w_ple_gate, w_ple_proj, final_norm):
    b, s, d = x.shape
    depth = p.shape[0]
    assert depth == 1 and d == D_MODEL and w_in.shape[-1] == 8 * GROUP
    assert s % PROJ_TM == 0 and s % ATTN_TQ == 0 and (b * s) % OUT_TM == 0
    bf16 = jnp.bfloat16
    lam_init = 0.8 - 0.6 * math.exp(-0.3 * 0)

    w_bf = w_in[0].astype(bf16)
    wvt_bf = w_in[0][:, 6 * GROUP:7 * GROUP].T.astype(bf16)
    half = ROT_DIM // 2
    invf = (ROPE_THETA ** (-jnp.arange(half, dtype=jnp.float32) / half))[:, None]
    yconv, q, k, vt, gate = _projection(
        x, positions[:, None, :], norm_mix[0][None, :], w_bf, wvt_bf,
        conv_w[0], conv_b[0][None, :], invf)

    lam_params = jnp.stack([lambda_q1[0], lambda_k1[0], lambda_q2[0], lambda_k2[0]])
    yattn = _attention(q, k, vt, gate, lam_params, subln_g[0][None, :], lam_init)

    out = _output(
        x.reshape(b * s, d), yconv.reshape(b * s, CONV_WIDTH), yattn.reshape(b * s, ATTN_WIDTH),
        p[0].reshape(b * s, PLE_DIM), w_out[0].astype(bf16), norm_ple[0][None, :],
        w_ple_gate[0].astype(bf16), w_ple_proj[0].astype(bf16), final_norm[None, :])
    return out.reshape(b, s, d)
```

```python
import functools
import math

import jax
import jax.numpy as jnp
from jax import lax
from jax.experimental import pallas as pl
from jax.experimental.pallas import tpu as pltpu

D_MODEL = 1024
CHUNK = 64
PLE_DIM = 256
CONV_WIDTH = 512
CONV_K = 3
ATTN_WIDTH = 512
N_HEADS = 4
HEAD_DIM = 64
V_DIM = 2 * HEAD_DIM
V_EXT = V_DIM + 16
ROT_DIM = HEAD_DIM // 4
ROPE_THETA = 500000.0
EPS = 1e-6
SUBLN_EPS = 1e-5
GROUP = 512
LANES = 128
SUBLANES = 8
NEG = -1e30

PROJ_TM = 512
PROJ_SUB = 256
ATTN_TQ = 512
ATTN_TK = 256
OUT_TM = 512
VMEM_LIMIT = 56 * 1024 * 1024

_NT = (((1,), (1,)), ((), ()))


def _const_spec(shape):
    return pl.BlockSpec(shape, lambda *_: (0,) * len(shape), pipeline_mode=pl.Buffered(1))


def _proj_kernel(x_ref, pos_ref, g_ref, w_ref, wvt_ref, cw_ref, cb_ref, invf_ref,
                 yconv_ref, q_ref, k_ref, vt_ref, gate_ref, ubuf_ref, *, tm, sub):
    si = pl.program_id(1)
    bf16 = jnp.bfloat16

    @pl.when(si == 0)
    def _():
        ubuf_ref[...] = jnp.zeros_like(ubuf_ref)

    prev = ubuf_ref[...]
    row = lax.broadcasted_iota(jnp.int32, (SUBLANES, 1), 0)
    lane = lax.broadcasted_iota(jnp.int32, (1, LANES), 1)
    low = (lane % HEAD_DIM) < (ROT_DIM // 2)
    fill = HEAD_DIM - ROT_DIM
    one, zero = jnp.ones((fill, sub), jnp.float32), jnp.zeros((fill, sub), jnp.float32)
    ones_rows = jnp.ones((V_EXT - V_DIM, sub), bf16)

    for r0 in range(0, tm, sub):
        rows = slice(r0, r0 + sub)
        x = x_ref[0, rows, :]
        ms = jnp.mean(x * x, axis=-1, keepdims=True)
        u = (x * lax.rsqrt(ms + EPS) * g_ref[...]).astype(bf16)

        def proj(c):
            return jnp.dot(u, w_ref[:, c * GROUP:(c + 1) * GROUP], preferred_element_type=jnp.float32)

        uc = proj(2) * proj(0)

        def shifted(k):
            r = pltpu.roll(uc, k, 0)
            head = jnp.where(row < k, pltpu.roll(prev, k, 0), r[0:SUBLANES, :])
            return jnp.concatenate([head, r[SUBLANES:, :]], axis=0)

        conv = (cw_ref[0:1, :] * shifted(2) + cw_ref[1:2, :] * shifted(1)
                + cw_ref[2:3, :] * uc + cb_ref[...])
        prev = uc[sub - SUBLANES:, :]
        cz = proj(3)
        yconv_ref[0, rows, :] = (proj(1) * conv * (cz * jax.nn.sigmoid(cz))).astype(bf16)

        ang = invf_ref[...] * pos_ref[0, :, rows].astype(jnp.float32)
        c8, s8 = jnp.cos(ang), jnp.sin(ang)
        cos = jnp.concatenate([c8, c8, one] * 2, axis=0).T
        sin = jnp.concatenate([-s8, s8, zero] * 2, axis=0).T

        def rope(t, scale):
            outs = []
            for h in range(N_HEADS):
                th = t[:, h * LANES:(h + 1) * LANES]
                partner = jnp.where(low, pltpu.roll(th, LANES - ROT_DIM // 2, 1),
                                    pltpu.roll(th, ROT_DIM // 2, 1))
                r = th * cos + partner * sin
                outs.append(r * scale if scale != 1.0 else r)
            return jnp.concatenate(outs, axis=1)

        q_ref[0, rows, :] = rope(proj(4), HEAD_DIM ** -0.5 * math.log2(math.e)).astype(bf16)
        k_ref[0, rows, :] = rope(proj(5), 1.0).astype(bf16)
        az = proj(7)
        gate_ref[0, rows, :] = (az * jax.nn.sigmoid(az)).astype(bf16)
        vt = lax.dot_general(wvt_ref[...], u, _NT, preferred_element_type=jnp.float32).astype(bf16)
        for h in range(N_HEADS):
            vt_ref[0, h * V_EXT:h * V_EXT + V_DIM, rows] = vt[h * V_DIM:(h + 1) * V_DIM, :]
            vt_ref[0, h * V_EXT + V_DIM:(h + 1) * V_EXT, rows] = ones_rows

    ubuf_ref[...] = prev


def _projection(x, pos3, g, w_bf, wvt_bf, cw, cb, invf):
    b, s, d = x.shape
    tm = PROJ_TM
    tok = lambda bi, si: (bi, si, 0)
    out_tok = jax.ShapeDtypeStruct((b, s, GROUP), jnp.bfloat16)
    vt_rows = N_HEADS * V_EXT
    return pl.pallas_call(
        functools.partial(_proj_kernel, tm=tm, sub=PROJ_SUB),
        grid=(b, s // tm),
        in_specs=[
            pl.BlockSpec((1, tm, d), tok),
            pl.BlockSpec((1, 1, tm), lambda bi, si: (bi, 0, si)),
            _const_spec((1, d)),
            _const_spec(w_bf.shape),
            _const_spec(wvt_bf.shape),
            _const_spec(cw.shape),
            _const_spec(cb.shape),
            _const_spec(invf.shape),
        ],
        out_specs=[
            pl.BlockSpec((1, tm, GROUP), tok),
            pl.BlockSpec((1, tm, GROUP), tok),
            pl.BlockSpec((1, tm, GROUP), tok),
            pl.BlockSpec((1, vt_rows, tm), lambda bi, si: (bi, 0, si)),
            pl.BlockSpec((1, tm, GROUP), tok),
        ],
        out_shape=[out_tok, out_tok, out_tok,
                   jax.ShapeDtypeStruct((b, vt_rows, s), jnp.bfloat16), out_tok],
        scratch_shapes=[pltpu.VMEM((SUBLANES, GROUP), jnp.float32)],
        compiler_params=pltpu.CompilerParams(
            dimension_semantics=("parallel", "arbitrary"), vmem_limit_bytes=VMEM_LIMIT),
        name="proj_conv_rope",
    )(x, pos3, g, w_bf, wvt_bf, cw, cb, invf)


def _attn_kernel(q_ref, qn_ref, k_ref, kn_ref, vt_ref, gate_ref, lam_ref, g_ref, y_ref,
                 s_ref, mt_ref, qz_ref, acc_ref, m_ref, *, tq, tk, lam_init):
    qt = pl.program_id(1)
    first_step = jnp.logical_and(pl.program_id(0) == 0, qt == 0)
    n_chain = 2 * N_HEADS
    heads = [slice(h * LANES, (h + 1) * LANES) for h in range(N_HEADS)]
    lane = lax.broadcasted_iota(jnp.int32, (1, LANES), 1)
    every = slice(0, tq)
    upper = slice(tq // 2, tq)

    def tail_mask(key_tile, cols=every):
        n = cols.stop - cols.start
        kchunk = lax.broadcasted_iota(jnp.int32, (tk, n), 0) // CHUNK + key_tile * (tk // CHUNK)
        qchunk = (lax.broadcasted_iota(jnp.int32, (tk, n), 1) + cols.start) // CHUNK
        return kchunk <= qchunk

    def split_queries(src_ref):
        for h, hs in enumerate(heads):
            qh = src_ref[0, :, hs]
            zero = jnp.zeros_like(qh)
            qz_ref[2 * h] = jnp.where(lane < HEAD_DIM, qh, zero)
            qz_ref[2 * h + 1] = jnp.where(lane >= HEAD_DIM, qh, zero)

    acc_ref[...] = jnp.zeros_like(acc_ref)
    m_ref[...] = jnp.full_like(m_ref, NEG)

    def qk(j, c, slot, mask, cols=every, keys_ref=None):
        if keys_ref is None:
            kh = k_ref[0, pl.ds(pl.multiple_of(j * tk, tk), tk), heads[c // 2]]
        else:
            kh = keys_ref[0, :, heads[c // 2]]
        st = lax.dot_general(kh, qz_ref[c, cols, :], _NT, preferred_element_type=jnp.float32)
        if mask is not None:
            st = jnp.where(mask, st, NEG)
        s_ref[slot, c, :, cols] = st
        mt_ref[slot, c, :, cols] = jnp.max(st, axis=0, keepdims=True)

    def softmax_pv(j, c, slot, cols=every):
        off = pl.multiple_of(j * tk, tk)
        m_old = m_ref[c, :, cols]
        m_new = jnp.maximum(m_old, mt_ref[slot, c, :, cols])
        alpha = jnp.exp2(m_old - m_new)
        p = jnp.exp2(s_ref[slot, c, :, cols] - m_new)
        m_ref[c, :, cols] = m_new
        h = c // 2
        vh = vt_ref[0, h * V_EXT:(h + 1) * V_EXT, pl.ds(off, tk)]
        acc_ref[c, :, cols] = alpha * acc_ref[c, :, cols] + jnp.dot(
            vh, p.astype(jnp.bfloat16), preferred_element_type=jnp.float32)

    lp = lam_ref[...]
    lam = (jnp.exp(jnp.sum(lp[0:1] * lp[1:2], axis=1, keepdims=True))
           - jnp.exp(jnp.sum(lp[2:3] * lp[3:4], axis=1, keepdims=True)) + lam_init)

    def finalize(h):
        hs = heads[h]
        a1, a2 = acc_ref[2 * h], acc_ref[2 * h + 1]
        o = (a1[:V_DIM] * (1.0 / a1[V_DIM:V_DIM + 1])
             - a2[:V_DIM] * (lam / a2[V_DIM:V_DIM + 1]))
        on = o * lax.rsqrt(jnp.mean(o * o, axis=0, keepdims=True) + SUBLN_EPS)
        y = on.T * (g_ref[...] * (1.0 - lam_init)) * gate_ref[0, :, hs].astype(jnp.float32)
        y_ref[0, :, hs] = y.astype(jnp.bfloat16)

    def stage(j, slot, has_next=True, next_mask=None, cols=every, next_cols=every, last=False):
        if last:
            split_queries(qn_ref)
        for c in range(n_chain):
            if has_next:
                qk(j + 1, c, 1 - slot, next_mask, next_cols)
            if last:
                qk(0, c, 1 - slot, None, keys_ref=kn_ref)
            softmax_pv(j, c, slot, cols)
            if last and c % 2 == 1:
                finalize(c // 2)

    @pl.when(first_step)
    def _():
        split_queries(q_ref)
        for c in range(n_chain):
            qk(0, c, 0, None)

    @pl.when(qt == 0)
    def _():
        mask = tail_mask(0)
        for c in range(n_chain):
            st = jnp.where(mask, s_ref[0, c], NEG)
            s_ref[0, c] = st
            mt_ref[0, c] = jnp.max(st, axis=0, keepdims=True)

    def pair(jj, carry):
        stage(2 * jj, 0)
        stage(2 * jj + 1, 1)
        return carry

    lax.fori_loop(0, jnp.maximum(qt - 1, 0), pair, 0)

    @pl.when(qt > 0)
    def _():
        stage(2 * qt - 2, 0)
        stage(2 * qt - 1, 1, next_mask=tail_mask(0))

    stage(2 * qt, 0, next_mask=tail_mask(1, upper), next_cols=upper)
    stage(2 * qt + 1, 1, has_next=False, cols=upper, last=True)


def _attention(q, k, vt, gate, lam_params, subln_g, lam_init):
    b, s, w = q.shape
    tq, tk = ATTN_TQ, ATTN_TK
    assert tq == 2 * tk and tk % CHUNK == 0
    n_chain = 2 * N_HEADS
    f32 = jnp.float32
    nq = s // tq
    qtile = lambda bi, qi: (bi, qi, 0)
    next_b = lambda bi, qi: jnp.minimum(bi + (qi + 1) // nq, b - 1)
    return pl.pallas_call(
        functools.partial(_attn_kernel, tq=tq, tk=tk, lam_init=lam_init),
        grid=(b, nq),
        in_specs=[
            pl.BlockSpec((1, tq, w), qtile),
            pl.BlockSpec((1, tq, w), lambda bi, qi: (next_b(bi, qi), (qi + 1) % nq, 0)),
            pl.BlockSpec((1, s, w), lambda bi, qi: (bi, 0, 0)),
            pl.BlockSpec((1, tk, w), lambda bi, qi: (next_b(bi, qi), 0, 0)),
            pl.BlockSpec((1, vt.shape[1], s), lambda bi, qi: (bi, 0, 0)),
            pl.BlockSpec((1, tq, w), qtile),
            _const_spec(lam_params.shape),
            _const_spec(subln_g.shape),
        ],
        out_specs=pl.BlockSpec((1, tq, w), qtile),
        out_shape=jax.ShapeDtypeStruct((b, s, w), jnp.bfloat16),
        scratch_shapes=[
            pltpu.VMEM((2, n_chain, tk, tq), f32),
            pltpu.VMEM((2, n_chain, 1, tq), f32),
            pltpu.VMEM((n_chain, tq, LANES), jnp.bfloat16),
            pltpu.VMEM((n_chain, V_EXT, tq), f32),
            pltpu.VMEM((n_chain, 1, tq), f32),
        ],
        compiler_params=pltpu.CompilerParams(
            dimension_semantics=("arbitrary", "arbitrary"), vmem_limit_bytes=VMEM_LIMIT),
        name="diff_attention",
    )(q, q, k, k, vt, gate, lam_params, subln_g)


def _rms(h, g):
    return h * lax.rsqrt(jnp.mean(h * h, axis=-1, keepdims=True) + EPS) * g


def _out_kernel(x_ref, yc_ref, ya_ref, p_ref, wo_ref, np_ref, wg_ref, wp_ref, fn_ref, o_ref):
    f32 = jnp.float32
    mix = (jnp.dot(yc_ref[...], wo_ref[0:CONV_WIDTH, :], preferred_element_type=f32)
           + jnp.dot(ya_ref[...], wo_ref[CONV_WIDTH:, :], preferred_element_type=f32))
    h = x_ref[...] + mix
    r = _rms(h, np_ref[...]).astype(jnp.bfloat16)
    gate = jax.nn.sigmoid(jnp.dot(r, wg_ref[...], preferred_element_type=f32))
    pp = jnp.dot(p_ref[...].astype(jnp.bfloat16), wp_ref[...], preferred_element_type=f32)
    h = h + gate * pp
    o_ref[...] = _rms(h, fn_ref[...])


def _output(x2, yc2, ya2, p2, wo_bf, norm_ple, wg_bf, wp_bf, final_norm):
    t, d = x2.shape
    tm = OUT_TM
    row = lambda i: (i, 0)
    return pl.pallas_call(
        _out_kernel,
        grid=(t // tm,),
        in_specs=[
            pl.BlockSpec((tm, d), row),
            pl.BlockSpec((tm, CONV_WIDTH), row),
            pl.BlockSpec((tm, ATTN_WIDTH), row),
            pl.BlockSpec((tm, PLE_DIM), row),
            _const_spec(wo_bf.shape),
            _const_spec(norm_ple.shape),
            _const_spec(wg_bf.shape),
            _const_spec(wp_bf.shape),
            _const_spec(final_norm.shape),
        ],
        out_specs=pl.BlockSpec((tm, d), row),
        out_shape=jax.ShapeDtypeStruct((t, d), jnp.float32),
        compiler_params=pltpu.CompilerParams(
            dimension_semantics=("parallel",), vmem_limit_bytes=VMEM_LIMIT),
        name="out_ple_norm",
    )(x2, yc2, ya2, p2, wo_bf, norm_ple, wg_bf, wp_bf, final_norm)


def kernel(x, p, positions, norm_mix, w_in, conv_w, conv_b, lambda_q1, lambda_k1, lambda_q2,
           lambda_k2, subln_g, w_out, norm_ple, w_ple_gate, w_ple_proj, final_norm):
    b, s, d = x.shape
    depth = p.shape[0]
    assert depth == 1 and d == D_MODEL and w_in.shape[-1] == 8 * GROUP
    assert s % PROJ_TM == 0 and s % ATTN_TQ == 0 and (b * s) % OUT_TM == 0
    bf16 = jnp.bfloat16
    lam_init = 0.8 - 0.6 * math.exp(-0.3 * 0)

    w_bf = w_in[0].astype(bf16)
    wvt_bf = w_in[0][:, 6 * GROUP:7 * GROUP].T.astype(bf16)
    half = ROT_DIM // 2
    invf = (ROPE_THETA ** (-jnp.arange(half, dtype=jnp.float32) / half))[:, None]
    yconv, q, k, vt, gate = _projection(
        x, positions[:, None, :], norm_mix[0][None, :], w_bf, wvt_bf,
        conv_w[0], conv_b[0][None, :], invf)

    lam_params = jnp.stack([lambda_q1[0], lambda_k1[0], lambda_q2[0], lambda_k2[0]])
    yattn = _attention(q, k, vt, gate, lam_params, subln_g[0][None, :], lam_init)

    out = _output(
        x.reshape(b * s, d), yconv.reshape(b * s, CONV_WIDTH), yattn.reshape(b * s, ATTN_WIDTH),
        p[0].reshape(b * s, PLE_DIM), w_out[0].astype(bf16), norm_ple[0][None, :],
        w_ple_gate[0].astype(bf16), w_ple_proj[0].astype(bf16), final_norm[None, :])
    return out.reshape(b, s, d)
```

```python
import functools
import math

import jax
import jax.numpy as jnp
from jax import lax
from jax.experimental import pallas as pl
from jax.experimental.pallas import tpu as pltpu

D_MODEL = 1024
CHUNK = 64
PLE_DIM = 256
CONV_WIDTH = 512
CONV_K = 3
ATTN_WIDTH = 512
N_HEADS = 4
HEAD_DIM = 64
V_DIM = 2 * HEAD_DIM
V_EXT = V_DIM + 16
ROT_DIM = HEAD_DIM // 4
ROPE_THETA = 500000.0
EPS = 1e-6
SUBLN_EPS = 1e-5
GROUP = 512
V_GROUP = 6
LANES = 128
SUBLANES = 8
NEG = -1e30

PROJ_TM = 512
PROJ_SUB = 256
ATTN_TQ = 512
ATTN_TK = 256
OUT_TM = 512
VMEM_LIMIT = 56 * 1024 * 1024

_NT = (((1,), (1,)), ((), ()))


def _const_spec(shape):
    return pl.BlockSpec(shape, lambda *_: (0,) * len(shape), pipeline_mode=pl.Buffered(1))


def _proj_kernel(x_ref, pos_ref, g_ref, w32_ref, cw_ref, cb_ref, invf_ref,
                 yconv_ref, q_ref, k_ref, vt_ref, gate_ref, w_ref, wvt_ref, ubuf_ref, *, tm, sub):
    si = pl.program_id(1)
    bf16 = jnp.bfloat16

    @pl.when(jnp.logical_and(pl.program_id(0) == 0, si == 0))
    def _():
        for c in range(w32_ref.shape[1] // GROUP):
            cols = slice(c * GROUP, (c + 1) * GROUP)
            w_ref[:, cols] = w32_ref[:, cols].astype(bf16)
        wvt_ref[...] = w32_ref[:, V_GROUP * GROUP:(V_GROUP + 1) * GROUP].T.astype(bf16)

    @pl.when(si == 0)
    def _():
        ubuf_ref[...] = jnp.zeros_like(ubuf_ref)

    prev = ubuf_ref[...]
    row = lax.broadcasted_iota(jnp.int32, (SUBLANES, 1), 0)
    lane = lax.broadcasted_iota(jnp.int32, (1, LANES), 1)
    low = (lane % HEAD_DIM) < (ROT_DIM // 2)
    fill = HEAD_DIM - ROT_DIM
    one, zero = jnp.ones((fill, sub), jnp.float32), jnp.zeros((fill, sub), jnp.float32)
    ones_rows = jnp.ones((V_EXT - V_DIM, sub), bf16)

    for r0 in range(0, tm, sub):
        rows = slice(r0, r0 + sub)
        x = x_ref[0, rows, :]
        ms = jnp.mean(x * x, axis=-1, keepdims=True)
        u = (x * lax.rsqrt(ms + EPS) * g_ref[...]).astype(bf16)

        def proj(c):
            return jnp.dot(u, w_ref[:, c * GROUP:(c + 1) * GROUP], preferred_element_type=jnp.float32)

        uc = proj(2) * proj(0)

        def shifted(k):
            r = pltpu.roll(uc, k, 0)
            head = jnp.where(row < k, pltpu.roll(prev, k, 0), r[0:SUBLANES, :])
            return jnp.concatenate([head, r[SUBLANES:, :]], axis=0)

        conv = (cw_ref[0:1, :] * shifted(2) + cw_ref[1:2, :] * shifted(1)
                + cw_ref[2:3, :] * uc + cb_ref[...])
        prev = uc[sub - SUBLANES:, :]
        cz = proj(3)
        yconv_ref[0, rows, :] = (proj(1) * conv * (cz * jax.nn.sigmoid(cz))).astype(bf16)

        ang = invf_ref[...] * pos_ref[0, :, rows].astype(jnp.float32)
        c8, s8 = jnp.cos(ang), jnp.sin(ang)
        cos = jnp.concatenate([c8, c8, one] * 2, axis=0).T
        sin = jnp.concatenate([-s8, s8, zero] * 2, axis=0).T

        def rope(t, scale):
            outs = []
            for h in range(N_HEADS):
                th = t[:, h * LANES:(h + 1) * LANES]
                partner = jnp.where(low, pltpu.roll(th, LANES - ROT_DIM // 2, 1),
                                    pltpu.roll(th, ROT_DIM // 2, 1))
                r = th * cos + partner * sin
                outs.append(r * scale if scale != 1.0 else r)
            return jnp.concatenate(outs, axis=1)

        q_ref[0, rows, :] = rope(proj(4), HEAD_DIM ** -0.5 * math.log2(math.e)).astype(bf16)
        k_ref[0, rows, :] = rope(proj(5), 1.0).astype(bf16)
        az = proj(7)
        gate_ref[0, rows, :] = (az * jax.nn.sigmoid(az)).astype(bf16)
        vt = lax.dot_general(wvt_ref[...], u, _NT, preferred_element_type=jnp.float32).astype(bf16)
        for h in range(N_HEADS):
            vt_ref[0, h * V_EXT:h * V_EXT + V_DIM, rows] = vt[h * V_DIM:(h + 1) * V_DIM, :]
            vt_ref[0, h * V_EXT + V_DIM:(h + 1) * V_EXT, rows] = ones_rows

    ubuf_ref[...] = prev


def _projection(x, pos3, g, w32, cw, cb, invf):
    b, s, d = x.shape
    tm = PROJ_TM
    tok = lambda bi, si: (bi, si, 0)
    out_tok = jax.ShapeDtypeStruct((b, s, GROUP), jnp.bfloat16)
    vt_rows = N_HEADS * V_EXT
    return pl.pallas_call(
        functools.partial(_proj_kernel, tm=tm, sub=PROJ_SUB),
        grid=(b, s // tm),
        in_specs=[
            pl.BlockSpec((1, tm, d), tok),
            pl.BlockSpec((1, 1, tm), lambda bi, si: (bi, 0, si)),
            _const_spec((1, d)),
            _const_spec(w32.shape),
            _const_spec(cw.shape),
            _const_spec(cb.shape),
            _const_spec(invf.shape),
        ],
        out_specs=[
            pl.BlockSpec((1, tm, GROUP), tok),
            pl.BlockSpec((1, tm, GROUP), tok),
            pl.BlockSpec((1, tm, GROUP), tok),
            pl.BlockSpec((1, vt_rows, tm), lambda bi, si: (bi, 0, si)),
            pl.BlockSpec((1, tm, GROUP), tok),
        ],
        out_shape=[out_tok, out_tok, out_tok,
                   jax.ShapeDtypeStruct((b, vt_rows, s), jnp.bfloat16), out_tok],
        scratch_shapes=[
            pltpu.VMEM(w32.shape, jnp.bfloat16),
            pltpu.VMEM((GROUP, d), jnp.bfloat16),
            pltpu.VMEM((SUBLANES, GROUP), jnp.float32),
        ],
        compiler_params=pltpu.CompilerParams(
            dimension_semantics=("arbitrary", "arbitrary"), vmem_limit_bytes=VMEM_LIMIT),
        name="proj_conv_rope",
    )(x, pos3, g, w32, cw, cb, invf)


def _attn_kernel(q_ref, qn_ref, k_ref, kn_ref, vt_ref, gate_ref, lam_ref, g_ref, y_ref,
                 s_ref, mt_ref, qz_ref, acc_ref, m_ref, *, tq, tk, lam_init):
    qt = pl.program_id(1)
    first_step = jnp.logical_and(pl.program_id(0) == 0, qt == 0)
    n_chain = 2 * N_HEADS
    heads = [slice(h * LANES, (h + 1) * LANES) for h in range(N_HEADS)]
    lane = lax.broadcasted_iota(jnp.int32, (1, LANES), 1)
    every = slice(0, tq)
    upper = slice(tq // 2, tq)

    def tail_mask(key_tile, cols=every):
        n = cols.stop - cols.start
        kchunk = lax.broadcasted_iota(jnp.int32, (tk, n), 0) // CHUNK + key_tile * (tk // CHUNK)
        qchunk = (lax.broadcasted_iota(jnp.int32, (tk, n), 1) + cols.start) // CHUNK
        return kchunk <= qchunk

    def split_queries(src_ref):
        for h, hs in enumerate(heads):
            qh = src_ref[0, :, hs]
            zero = jnp.zeros_like(qh)
            qz_ref[2 * h] = jnp.where(lane < HEAD_DIM, qh, zero)
            qz_ref[2 * h + 1] = jnp.where(lane >= HEAD_DIM, qh, zero)

    acc_ref[...] = jnp.zeros_like(acc_ref)
    m_ref[...] = jnp.full_like(m_ref, NEG)

    def qk(j, c, slot, mask, cols=every, keys_ref=None):
        if keys_ref is None:
            kh = k_ref[0, pl.ds(pl.multiple_of(j * tk, tk), tk), heads[c // 2]]
        else:
            kh = keys_ref[0, :, heads[c // 2]]
        st = lax.dot_general(kh, qz_ref[c, cols, :], _NT, preferred_element_type=jnp.float32)
        if mask is not None:
            st = jnp.where(mask, st, NEG)
        s_ref[slot, c, :, cols] = st
        mt_ref[slot, c, :, cols] = jnp.max(st, axis=0, keepdims=True)

    def softmax_pv(j, c, slot, cols=every):
        off = pl.multiple_of(j * tk, tk)
        m_old = m_ref[c, :, cols]
        m_new = jnp.maximum(m_old, mt_ref[slot, c, :, cols])
        alpha = jnp.exp2(m_old - m_new)
        p = jnp.exp2(s_ref[slot, c, :, cols] - m_new)
        m_ref[c, :, cols] = m_new
        h = c // 2
        vh = vt_ref[0, h * V_EXT:(h + 1) * V_EXT, pl.ds(off, tk)]
        acc_ref[c, :, cols] = alpha * acc_ref[c, :, cols] + jnp.dot(
            vh, p.astype(jnp.bfloat16), preferred_element_type=jnp.float32)

    lp = lam_ref[...]
    lam = (jnp.exp(jnp.sum(lp[0:1] * lp[1:2], axis=1, keepdims=True))
           - jnp.exp(jnp.sum(lp[2:3] * lp[3:4], axis=1, keepdims=True)) + lam_init)

    def finalize(h):
        hs = heads[h]
        a1, a2 = acc_ref[2 * h], acc_ref[2 * h + 1]
        o = (a1[:V_DIM] * (1.0 / a1[V_DIM:V_DIM + 1])
             - a2[:V_DIM] * (lam / a2[V_DIM:V_DIM + 1]))
        on = o * lax.rsqrt(jnp.mean(o * o, axis=0, keepdims=True) + SUBLN_EPS)
        y = on.T * (g_ref[...] * (1.0 - lam_init)) * gate_ref[0, :, hs].astype(jnp.float32)
        y_ref[0, :, hs] = y.astype(jnp.bfloat16)

    def stage(j, slot, has_next=True, next_mask=None, cols=every, next_cols=every, last=False):
        if last:
            split_queries(qn_ref)
        for c in range(n_chain):
            if has_next:
                qk(j + 1, c, 1 - slot, next_mask, next_cols)
            if last:
                qk(0, c, 1 - slot, None, keys_ref=kn_ref)
            softmax_pv(j, c, slot, cols)
            if last and c % 2 == 1:
                finalize(c // 2)

    @pl.when(first_step)
    def _():
        split_queries(q_ref)
        for c in range(n_chain):
            qk(0, c, 0, None)

    @pl.when(qt == 0)
    def _():
        mask = tail_mask(0)
        for c in range(n_chain):
            st = jnp.where(mask, s_ref[0, c], NEG)
            s_ref[0, c] = st
            mt_ref[0, c] = jnp.max(st, axis=0, keepdims=True)

    def pair(jj, carry):
        stage(2 * jj, 0)
        stage(2 * jj + 1, 1)
        return carry

    lax.fori_loop(0, jnp.maximum(qt - 1, 0), pair, 0)

    @pl.when(qt > 0)
    def _():
        stage(2 * qt - 2, 0)
        stage(2 * qt - 1, 1, next_mask=tail_mask(0))

    stage(2 * qt, 0, next_mask=tail_mask(1, upper), next_cols=upper)
    stage(2 * qt + 1, 1, has_next=False, cols=upper, last=True)


def _attention(q, k, vt, gate, lam_params, subln_g, lam_init):
    b, s, w = q.shape
    tq, tk = ATTN_TQ, ATTN_TK
    assert tq == 2 * tk and tk % CHUNK == 0
    n_chain = 2 * N_HEADS
    f32 = jnp.float32
    nq = s // tq
    qtile = lambda bi, qi: (bi, qi, 0)
    next_b = lambda bi, qi: jnp.minimum(bi + (qi + 1) // nq, b - 1)
    return pl.pallas_call(
        functools.partial(_attn_kernel, tq=tq, tk=tk, lam_init=lam_init),
        grid=(b, nq),
        in_specs=[
            pl.BlockSpec((1, tq, w), qtile),
            pl.BlockSpec((1, tq, w), lambda bi, qi: (next_b(bi, qi), (qi + 1) % nq, 0)),
            pl.BlockSpec((1, s, w), lambda bi, qi: (bi, 0, 0)),
            pl.BlockSpec((1, tk, w), lambda bi, qi: (next_b(bi, qi), 0, 0)),
            pl.BlockSpec((1, vt.shape[1], s), lambda bi, qi: (bi, 0, 0)),
            pl.BlockSpec((1, tq, w), qtile),
            _const_spec(lam_params.shape),
            _const_spec(subln_g.shape),
        ],
        out_specs=pl.BlockSpec((1, tq, w), qtile),
        out_shape=jax.ShapeDtypeStruct((b, s, w), jnp.bfloat16),
        scratch_shapes=[
            pltpu.VMEM((2, n_chain, tk, tq), f32),
            pltpu.VMEM((2, n_chain, 1, tq), f32),
            pltpu.VMEM((n_chain, tq, LANES), jnp.bfloat16),
            pltpu.VMEM((n_chain, V_EXT, tq), f32),
            pltpu.VMEM((n_chain, 1, tq), f32),
        ],
        compiler_params=pltpu.CompilerParams(
            dimension_semantics=("arbitrary", "arbitrary"), vmem_limit_bytes=VMEM_LIMIT),
        name="diff_attention",
    )(q, q, k, k, vt, gate, lam_params, subln_g)


def _rms(h, g):
    return h * lax.rsqrt(jnp.mean(h * h, axis=-1, keepdims=True) + EPS) * g


def _out_kernel(x_ref, yc_ref, ya_ref, p_ref, wo_ref, np_ref, wg_ref, wp_ref, fn_ref, o_ref):
    f32 = jnp.float32
    mix = (jnp.dot(yc_ref[...], wo_ref[0:CONV_WIDTH, :], preferred_element_type=f32)
           + jnp.dot(ya_ref[...], wo_ref[CONV_WIDTH:, :], preferred_element_type=f32))
    h = x_ref[...] + mix
    r = _rms(h, np_ref[...]).astype(jnp.bfloat16)
    gate = jax.nn.sigmoid(jnp.dot(r, wg_ref[...], preferred_element_type=f32))
    pp = jnp.dot(p_ref[...].astype(jnp.bfloat16), wp_ref[...], preferred_element_type=f32)
    h = h + gate * pp
    o_ref[...] = _rms(h, fn_ref[...])


def _output(x2, yc2, ya2, p2, wo_bf, norm_ple, wg_bf, wp_bf, final_norm):
    t, d = x2.shape
    tm = OUT_TM
    row = lambda i: (i, 0)
    return pl.pallas_call(
        _out_kernel,
        grid=(t // tm,),
        in_specs=[
            pl.BlockSpec((tm, d), row),
            pl.BlockSpec((tm, CONV_WIDTH), row),
            pl.BlockSpec((tm, ATTN_WIDTH), row),
            pl.BlockSpec((tm, PLE_DIM), row),
            _const_spec(wo_bf.shape),
            _const_spec(norm_ple.shape),
            _const_spec(wg_bf.shape),
            _const_spec(wp_bf.shape),
            _const_spec(final_norm.shape),
        ],
        out_specs=pl.BlockSpec((tm, d), row),
        out_shape=jax.ShapeDtypeStruct((t, d), jnp.float32),
        compiler_params=pltpu.CompilerParams(
            dimension_semantics=("parallel",), vmem_limit_bytes=VMEM_LIMIT),
        name="out_ple_norm",
    )(x2, yc2, ya2, p2, wo_bf, norm_ple, wg_bf, wp_bf, final_norm)


def kernel(x, p, positions, norm_mix, w_in, conv_w, conv_b, lambda_q1, lambda_k1, lambda_q2,
           lambda_k2, subln_g, w_out, norm_ple, w_ple_gate, w_ple_proj, final_norm):
    b, s, d = x.shape
    depth = p.shape[0]
    assert depth == 1 and d == D_MODEL and w_in.shape[-1] == 8 * GROUP
    assert s % PROJ_TM == 0 and s % ATTN_TQ == 0 and (b * s) % OUT_TM == 0
    bf16 = jnp.bfloat16
    lam_init = 0.8 - 0.6 * math.exp(-0.3 * 0)

    half = ROT_DIM // 2
    invf = (ROPE_THETA ** (-jnp.arange(half, dtype=jnp.float32) / half))[:, None]
    yconv, q, k, vt, gate = _projection(
        x, positions[:, None, :], norm_mix[0][None, :], w_in[0],
        conv_w[0], conv_b[0][None, :], invf)

    lam_params = jnp.stack([lambda_q1[0], lambda_k1[0], lambda_q2[0], lambda_k2[0]])
    yattn = _attention(q, k, vt, gate, lam_params, subln_g[0][None, :], lam_init)

    out = _output(
        x.reshape(b * s, d), yconv.reshape(b * s, CONV_WIDTH), yattn.reshape(b * s, ATTN_WIDTH),
        p[0].reshape(b * s, PLE_DIM), w_out[0].astype(bf16), norm_ple[0][None, :],
        w_ple_gate[0].astype(bf16), w_ple_proj[0].astype(bf16), final_norm[None, :])
    return out.reshape(b, s, d)
```

```python
import functools
import math

import jax
import jax.numpy as jnp
from jax import lax
from jax.experimental import pallas as pl
from jax.experimental.pallas import tpu as pltpu

D_MODEL = 1024
CHUNK = 64
PLE_DIM = 256
CONV_WIDTH = 512
CONV_K = 3
ATTN_WIDTH = 512
N_HEADS = 4
HEAD_DIM = 64
V_DIM = 2 * HEAD_DIM
V_EXT = V_DIM + 16
ROT_DIM = HEAD_DIM // 4
ROPE_THETA = 500000.0
EPS = 1e-6
SUBLN_EPS = 1e-5
GROUP = 512
V_GROUP = 6
LANES = 128
SUBLANES = 8
NEG = -1e30

PROJ_TM = 1024
PROJ_SUB = 256
ATTN_TQ = 512
ATTN_TK = 256
OUT_TM = 1024
VMEM_LIMIT = 56 * 1024 * 1024

_NT = (((1,), (1,)), ((), ()))


def _const_spec(shape):
    return pl.BlockSpec(shape, lambda *_: (0,) * len(shape), pipeline_mode=pl.Buffered(1))


def _proj_kernel(x_ref, pos_ref, g_ref, w32_ref, cw_ref, cb_ref, invf_ref,
                 yconv_ref, q_ref, k_ref, vt_ref, gate_ref, w_ref, wvt_ref, ubuf_ref, *, tm, sub):
    si = pl.program_id(1)
    bf16 = jnp.bfloat16

    @pl.when(jnp.logical_and(pl.program_id(0) == 0, si == 0))
    def _():
        for c in range(w32_ref.shape[1] // GROUP):
            cols = slice(c * GROUP, (c + 1) * GROUP)
            w_ref[:, cols] = w32_ref[:, cols].astype(bf16)
        wvt_ref[...] = w32_ref[:, V_GROUP * GROUP:(V_GROUP + 1) * GROUP].T.astype(bf16)

    @pl.when(si == 0)
    def _():
        ubuf_ref[...] = jnp.zeros_like(ubuf_ref)

    prev = ubuf_ref[...]
    row = lax.broadcasted_iota(jnp.int32, (SUBLANES, 1), 0)
    lane = lax.broadcasted_iota(jnp.int32, (1, LANES), 1)
    low = (lane % HEAD_DIM) < (ROT_DIM // 2)
    fill = HEAD_DIM - ROT_DIM
    one, zero = jnp.ones((fill, sub), jnp.float32), jnp.zeros((fill, sub), jnp.float32)
    ones_rows = jnp.ones((V_EXT - V_DIM, sub), bf16)

    for r0 in range(0, tm, sub):
        rows = slice(r0, r0 + sub)
        x = x_ref[0, rows, :]
        ms = jnp.mean(x * x, axis=-1, keepdims=True)
        u = (x * lax.rsqrt(ms + EPS) * g_ref[...]).astype(bf16)

        def proj(c):
            return jnp.dot(u, w_ref[:, c * GROUP:(c + 1) * GROUP], preferred_element_type=jnp.float32)

        uc = proj(2) * proj(0)

        def shifted(k):
            r = pltpu.roll(uc, k, 0)
            head = jnp.where(row < k, pltpu.roll(prev, k, 0), r[0:SUBLANES, :])
            return jnp.concatenate([head, r[SUBLANES:, :]], axis=0)

        conv = (cw_ref[0:1, :] * shifted(2) + cw_ref[1:2, :] * shifted(1)
                + cw_ref[2:3, :] * uc + cb_ref[...])
        prev = uc[sub - SUBLANES:, :]
        cz = proj(3)
        yconv_ref[0, rows, :] = (proj(1) * conv * (cz * jax.nn.sigmoid(cz))).astype(bf16)

        ang = invf_ref[...] * pos_ref[0, :, rows].astype(jnp.float32)
        c8, s8 = jnp.cos(ang), jnp.sin(ang)
        cos = jnp.concatenate([c8, c8, one] * 2, axis=0).T
        sin = jnp.concatenate([-s8, s8, zero] * 2, axis=0).T

        def rope(t, scale):
            outs = []
            for h in range(N_HEADS):
                th = t[:, h * LANES:(h + 1) * LANES]
                partner = jnp.where(low, pltpu.roll(th, LANES - ROT_DIM // 2, 1),
                                    pltpu.roll(th, ROT_DIM // 2, 1))
                r = th * cos + partner * sin
                outs.append(r * scale if scale != 1.0 else r)
            return jnp.concatenate(outs, axis=1)

        q_ref[0, rows, :] = rope(proj(4), HEAD_DIM ** -0.5 * math.log2(math.e)).astype(bf16)
        k_ref[0, rows, :] = rope(proj(5), 1.0).astype(bf16)
        az = proj(7)
        gate_ref[0, rows, :] = (az * jax.nn.sigmoid(az)).astype(bf16)
        vt = lax.dot_general(wvt_ref[...], u, _NT, preferred_element_type=jnp.float32).astype(bf16)
        for h in range(N_HEADS):
            vt_ref[0, h * V_EXT:h * V_EXT + V_DIM, rows] = vt[h * V_DIM:(h + 1) * V_DIM, :]
            vt_ref[0, h * V_EXT + V_DIM:(h + 1) * V_EXT, rows] = ones_rows

    ubuf_ref[...] = prev


def _projection(x, pos3, g, w32, cw, cb, invf):
    b, s, d = x.shape
    tm = PROJ_TM
    tok = lambda bi, si: (bi, si, 0)
    out_tok = jax.ShapeDtypeStruct((b, s, GROUP), jnp.bfloat16)
    vt_rows = N_HEADS * V_EXT
    return pl.pallas_call(
        functools.partial(_proj_kernel, tm=tm, sub=PROJ_SUB),
        grid=(b, s // tm),
        in_specs=[
            pl.BlockSpec((1, tm, d), tok),
            pl.BlockSpec((1, 1, tm), lambda bi, si: (bi, 0, si)),
            _const_spec((1, d)),
            _const_spec(w32.shape),
            _const_spec(cw.shape),
            _const_spec(cb.shape),
            _const_spec(invf.shape),
        ],
        out_specs=[
            pl.BlockSpec((1, tm, GROUP), tok),
            pl.BlockSpec((1, tm, GROUP), tok),
            pl.BlockSpec((1, tm, GROUP), tok),
            pl.BlockSpec((1, vt_rows, tm), lambda bi, si: (bi, 0, si)),
            pl.BlockSpec((1, tm, GROUP), tok),
        ],
        out_shape=[out_tok, out_tok, out_tok,
                   jax.ShapeDtypeStruct((b, vt_rows, s), jnp.bfloat16), out_tok],
        scratch_shapes=[
            pltpu.VMEM(w32.shape, jnp.bfloat16),
            pltpu.VMEM((GROUP, d), jnp.bfloat16),
            pltpu.VMEM((SUBLANES, GROUP), jnp.float32),
        ],
        compiler_params=pltpu.CompilerParams(
            dimension_semantics=("arbitrary", "arbitrary"), vmem_limit_bytes=VMEM_LIMIT),
        name="proj_conv_rope",
    )(x, pos3, g, w32, cw, cb, invf)


def _attn_kernel(q_ref, qn_ref, k_ref, kn_ref, vt_ref, gate_ref, lam_ref, g_ref, y_ref,
                 s_ref, mt_ref, qz_ref, acc_ref, m_ref, *, tq, tk, lam_init):
    qt = pl.program_id(1)
    first_step = jnp.logical_and(pl.program_id(0) == 0, qt == 0)
    n_chain = 2 * N_HEADS
    heads = [slice(h * LANES, (h + 1) * LANES) for h in range(N_HEADS)]
    lane = lax.broadcasted_iota(jnp.int32, (1, LANES), 1)
    every = slice(0, tq)
    upper = slice(tq // 2, tq)

    def tail_mask(key_tile, cols=every):
        n = cols.stop - cols.start
        kchunk = lax.broadcasted_iota(jnp.int32, (tk, n), 0) // CHUNK + key_tile * (tk // CHUNK)
        qchunk = (lax.broadcasted_iota(jnp.int32, (tk, n), 1) + cols.start) // CHUNK
        return kchunk <= qchunk

    def split_queries(src_ref):
        for h, hs in enumerate(heads):
            qh = src_ref[0, :, hs]
            zero = jnp.zeros_like(qh)
            qz_ref[2 * h] = jnp.where(lane < HEAD_DIM, qh, zero)
            qz_ref[2 * h + 1] = jnp.where(lane >= HEAD_DIM, qh, zero)

    acc_ref[...] = jnp.zeros_like(acc_ref)
    m_ref[...] = jnp.full_like(m_ref, NEG)

    def qk(j, c, slot, mask, cols=every, keys_ref=None):
        if keys_ref is None:
            kh = k_ref[0, pl.ds(pl.multiple_of(j * tk, tk), tk), heads[c // 2]]
        else:
            kh = keys_ref[0, :, heads[c // 2]]
        st = lax.dot_general(kh, qz_ref[c, cols, :], _NT, preferred_element_type=jnp.float32)
        if mask is not None:
            st = jnp.where(mask, st, NEG)
        s_ref[slot, c, :, cols] = st
        mt_ref[slot, c, :, cols] = jnp.max(st, axis=0, keepdims=True)

    def softmax_pv(j, c, slot, cols=every):
        off = pl.multiple_of(j * tk, tk)
        m_old = m_ref[c, :, cols]
        m_new = jnp.maximum(m_old, mt_ref[slot, c, :, cols])
        alpha = jnp.exp2(m_old - m_new)
        p = jnp.exp2(s_ref[slot, c, :, cols] - m_new)
        m_ref[c, :, cols] = m_new
        h = c // 2
        vh = vt_ref[0, h * V_EXT:(h + 1) * V_EXT, pl.ds(off, tk)]
        acc_ref[c, :, cols] = alpha * acc_ref[c, :, cols] + jnp.dot(
            vh, p.astype(jnp.bfloat16), preferred_element_type=jnp.float32)

    lp = lam_ref[...]
    lam = (jnp.exp(jnp.sum(lp[0:1] * lp[1:2], axis=1, keepdims=True))
           - jnp.exp(jnp.sum(lp[2:3] * lp[3:4], axis=1, keepdims=True)) + lam_init)

    def finalize(h):
        hs = heads[h]
        a1, a2 = acc_ref[2 * h], acc_ref[2 * h + 1]
        o = (a1[:V_DIM] * (1.0 / a1[V_DIM:V_DIM + 1])
             - a2[:V_DIM] * (lam / a2[V_DIM:V_DIM + 1]))
        on = o * lax.rsqrt(jnp.mean(o * o, axis=0, keepdims=True) + SUBLN_EPS)
        y = on.T * (g_ref[...] * (1.0 - lam_init)) * gate_ref[0, :, hs].astype(jnp.float32)
        y_ref[0, :, hs] = y.astype(jnp.bfloat16)

    def stage(j, slot, has_next=True, next_mask=None, cols=every, next_cols=every, last=False):
        if last:
            split_queries(qn_ref)
        for c in range(n_chain):
            if has_next:
                qk(j + 1, c, 1 - slot, next_mask, next_cols)
            if last:
                qk(0, c, 1 - slot, None, keys_ref=kn_ref)
            softmax_pv(j, c, slot, cols)
            if last and c % 2 == 1:
                finalize(c // 2)

    @pl.when(first_step)
    def _():
        split_queries(q_ref)
        for c in range(n_chain):
            qk(0, c, 0, None)

    @pl.when(qt == 0)
    def _():
        mask = tail_mask(0)
        for c in range(n_chain):
            st = jnp.where(mask, s_ref[0, c], NEG)
            s_ref[0, c] = st
            mt_ref[0, c] = jnp.max(st, axis=0, keepdims=True)

    n_pairs = jnp.maximum(qt - 1, 0)

    def quad(i, carry):
        for t in range(4):
            stage(4 * i + t, t % 2)
        return carry

    lax.fori_loop(0, n_pairs // 2, quad, 0)

    @pl.when(n_pairs % 2 == 1)
    def _():
        stage(2 * n_pairs - 2, 0)
        stage(2 * n_pairs - 1, 1)

    @pl.when(qt > 0)
    def _():
        stage(2 * qt - 2, 0)
        stage(2 * qt - 1, 1, next_mask=tail_mask(0))

    stage(2 * qt, 0, next_mask=tail_mask(1, upper), next_cols=upper)
    stage(2 * qt + 1, 1, has_next=False, cols=upper, last=True)


def _attention(q, k, vt, gate, lam_params, subln_g, lam_init):
    b, s, w = q.shape
    tq, tk = ATTN_TQ, ATTN_TK
    assert tq == 2 * tk and tk % CHUNK == 0
    n_chain = 2 * N_HEADS
    f32 = jnp.float32
    nq = s // tq
    qtile = lambda bi, qi: (bi, qi, 0)
    next_b = lambda bi, qi: jnp.minimum(bi + (qi + 1) // nq, b - 1)
    return pl.pallas_call(
        functools.partial(_attn_kernel, tq=tq, tk=tk, lam_init=lam_init),
        grid=(b, nq),
        in_specs=[
            pl.BlockSpec((1, tq, w), qtile),
            pl.BlockSpec((1, tq, w), lambda bi, qi: (next_b(bi, qi), (qi + 1) % nq, 0)),
            pl.BlockSpec((1, s, w), lambda bi, qi: (bi, 0, 0)),
            pl.BlockSpec((1, tk, w), lambda bi, qi: (next_b(bi, qi), 0, 0)),
            pl.BlockSpec((1, vt.shape[1], s), lambda bi, qi: (bi, 0, 0)),
            pl.BlockSpec((1, tq, w), qtile),
            _const_spec(lam_params.shape),
            _const_spec(subln_g.shape),
        ],
        out_specs=pl.BlockSpec((1, tq, w), qtile),
        out_shape=jax.ShapeDtypeStruct((b, s, w), jnp.bfloat16),
        scratch_shapes=[
            pltpu.VMEM((2, n_chain, tk, tq), f32),
            pltpu.VMEM((2, n_chain, 1, tq), f32),
            pltpu.VMEM((n_chain, tq, LANES), jnp.bfloat16),
            pltpu.VMEM((n_chain, V_EXT, tq), f32),
            pltpu.VMEM((n_chain, 1, tq), f32),
        ],
        compiler_params=pltpu.CompilerParams(
            dimension_semantics=("arbitrary", "arbitrary"), vmem_limit_bytes=VMEM_LIMIT),
        name="diff_attention",
    )(q, q, k, k, vt, gate, lam_params, subln_g)


def _rms(h, g):
    return h * lax.rsqrt(jnp.mean(h * h, axis=-1, keepdims=True) + EPS) * g


def _out_kernel(x_ref, yc_ref, ya_ref, p_ref, wo_ref, np_ref, wg_ref, wp_ref, fn_ref, o_ref):
    f32 = jnp.float32
    mix = (jnp.dot(yc_ref[...], wo_ref[0:CONV_WIDTH, :], preferred_element_type=f32)
           + jnp.dot(ya_ref[...], wo_ref[CONV_WIDTH:, :], preferred_element_type=f32))
    h = x_ref[...] + mix
    r = _rms(h, np_ref[...]).astype(jnp.bfloat16)
    gate = jax.nn.sigmoid(jnp.dot(r, wg_ref[...], preferred_element_type=f32))
    pp = jnp.dot(p_ref[...].astype(jnp.bfloat16), wp_ref[...], preferred_element_type=f32)
    h = h + gate * pp
    o_ref[...] = _rms(h, fn_ref[...])


def _output(x2, yc2, ya2, p2, wo_bf, norm_ple, wg_bf, wp_bf, final_norm):
    t, d = x2.shape
    tm = OUT_TM
    row = lambda i: (i, 0)
    return pl.pallas_call(
        _out_kernel,
        grid=(t // tm,),
        in_specs=[
            pl.BlockSpec((tm, d), row),
            pl.BlockSpec((tm, CONV_WIDTH), row),
            pl.BlockSpec((tm, ATTN_WIDTH), row),
            pl.BlockSpec((tm, PLE_DIM), row),
            _const_spec(wo_bf.shape),
            _const_spec(norm_ple.shape),
            _const_spec(wg_bf.shape),
            _const_spec(wp_bf.shape),
            _const_spec(final_norm.shape),
        ],
        out_specs=pl.BlockSpec((tm, d), row),
        out_shape=jax.ShapeDtypeStruct((t, d), jnp.float32),
        compiler_params=pltpu.CompilerParams(
            dimension_semantics=("parallel",), vmem_limit_bytes=VMEM_LIMIT),
        name="out_ple_norm",
    )(x2, yc2, ya2, p2, wo_bf, norm_ple, wg_bf, wp_bf, final_norm)


def kernel(x, p, positions, norm_mix, w_in, conv_w, conv_b, lambda_q1, lambda_k1, lambda_q2,
           lambda_k2, subln_g, w_out, norm_ple, w_ple_gate, w_ple_proj, final_norm):
    b, s, d = x.shape
    depth = p.shape[0]
    assert depth == 1 and d == D_MODEL and w_in.shape[-1] == 8 * GROUP
    assert s % PROJ_TM == 0 and s % ATTN_TQ == 0 and (b * s) % OUT_TM == 0
    bf16 = jnp.bfloat16
    lam_init = 0.8 - 0.6 * math.exp(-0.3 * 0)

    half = ROT_DIM // 2
    invf = (ROPE_THETA ** (-jnp.arange(half, dtype=jnp.float32) / half))[:, None]
    yconv, q, k, vt, gate = _projection(
        x, positions[:, None, :], norm_mix[0][None, :], w_in[0],
        conv_w[0], conv_b[0][None, :], invf)

    lam_params = jnp.stack([lambda_q1[0], lambda_k1[0], lambda_q2[0], lambda_k2[0]])
    yattn = _attention(q, k, vt, gate, lam_params, subln_g[0][None, :], lam_init)

    out = _output(
        x.reshape(b * s, d), yconv.reshape(b * s, CONV_WIDTH), yattn.reshape(b * s, ATTN_WIDTH),
        p[0].reshape(b * s, PLE_DIM), w_out[0].astype(bf16), norm_ple[0][None, :],
        w_ple_gate[0].astype(bf16), w_ple_proj[0].astype(bf16), final_norm[None, :])
    return out.reshape(b, s, d)
```

```python
import functools
import math

import jax
import jax.numpy as jnp
from jax import lax
from jax.experimental import pallas as pl
from jax.experimental.pallas import tpu as pltpu

D_MODEL = 1024
CHUNK = 64
PLE_DIM = 256
CONV_WIDTH = 512
CONV_K = 3
ATTN_WIDTH = 512
N_HEADS = 4
HEAD_DIM = 64
V_DIM = 2 * HEAD_DIM
V_EXT = V_DIM + 16
ROT_DIM = HEAD_DIM // 4
ROPE_THETA = 500000.0
EPS = 1e-6
SUBLN_EPS = 1e-5
GROUP = 512
V_GROUP = 6
LANES = 128
SUBLANES = 8
NEG = -1e30

PROJ_TM = 1024
PROJ_SUB = 256
ATTN_TQ = 512
ATTN_TK = 256
OUT_TM = 1024
OUT_SUB = 256
VMEM_LIMIT = 56 * 1024 * 1024

_NT = (((1,), (1,)), ((), ()))


def _const_spec(shape):
    return pl.BlockSpec(shape, lambda *_: (0,) * len(shape), pipeline_mode=pl.Buffered(1))


def _proj_kernel(x_ref, pos_ref, g_ref, w32_ref, cw_ref, cb_ref, invf_ref,
                 yconv_ref, q_ref, k_ref, vt_ref, gate_ref, w_ref, wvt_ref, ubuf_ref, *, tm, sub):
    si = pl.program_id(1)
    bf16 = jnp.bfloat16

    @pl.when(jnp.logical_and(pl.program_id(0) == 0, si == 0))
    def _():
        for c in range(w32_ref.shape[1] // GROUP):
            cols = slice(c * GROUP, (c + 1) * GROUP)
            w_ref[:, cols] = w32_ref[:, cols].astype(bf16)
        wvt_ref[...] = w32_ref[:, V_GROUP * GROUP:(V_GROUP + 1) * GROUP].T.astype(bf16)

    @pl.when(si == 0)
    def _():
        ubuf_ref[...] = jnp.zeros_like(ubuf_ref)

    prev = ubuf_ref[...]
    row = lax.broadcasted_iota(jnp.int32, (SUBLANES, 1), 0)
    lane = lax.broadcasted_iota(jnp.int32, (1, LANES), 1)
    low = (lane % HEAD_DIM) < (ROT_DIM // 2)
    fill = HEAD_DIM - ROT_DIM
    one, zero = jnp.ones((fill, sub), jnp.float32), jnp.zeros((fill, sub), jnp.float32)
    ones_rows = jnp.ones((V_EXT - V_DIM, sub), bf16)

    for r0 in range(0, tm, sub):
        rows = slice(r0, r0 + sub)
        x = x_ref[0, rows, :]
        ms = jnp.mean(x * x, axis=-1, keepdims=True)
        u = (x * lax.rsqrt(ms + EPS) * g_ref[...]).astype(bf16)

        def proj(c):
            return jnp.dot(u, w_ref[:, c * GROUP:(c + 1) * GROUP], preferred_element_type=jnp.float32)

        uc = proj(2) * proj(0)

        def shifted(k):
            r = pltpu.roll(uc, k, 0)
            head = jnp.where(row < k, pltpu.roll(prev, k, 0), r[0:SUBLANES, :])
            return jnp.concatenate([head, r[SUBLANES:, :]], axis=0)

        conv = (cw_ref[0:1, :] * shifted(2) + cw_ref[1:2, :] * shifted(1)
                + cw_ref[2:3, :] * uc + cb_ref[...])
        prev = uc[sub - SUBLANES:, :]
        cz = proj(3)
        yconv_ref[0, rows, :] = (proj(1) * conv * (cz * jax.nn.sigmoid(cz))).astype(bf16)

        ang = invf_ref[...] * pos_ref[0, :, rows].astype(jnp.float32)
        c8, s8 = jnp.cos(ang), jnp.sin(ang)
        cos = jnp.concatenate([c8, c8, one] * 2, axis=0).T
        sin = jnp.concatenate([-s8, s8, zero] * 2, axis=0).T

        def rope(t, scale):
            outs = []
            for h in range(N_HEADS):
                th = t[:, h * LANES:(h + 1) * LANES]
                partner = jnp.where(low, pltpu.roll(th, LANES - ROT_DIM // 2, 1),
                                    pltpu.roll(th, ROT_DIM // 2, 1))
                r = th * cos + partner * sin
                outs.append(r * scale if scale != 1.0 else r)
            return jnp.concatenate(outs, axis=1)

        q_ref[0, rows, :] = rope(proj(4), HEAD_DIM ** -0.5 * math.log2(math.e)).astype(bf16)
        k_ref[0, rows, :] = rope(proj(5), 1.0).astype(bf16)
        az = proj(7)
        gate_ref[0, rows, :] = (az * jax.nn.sigmoid(az)).astype(bf16)
        vt = lax.dot_general(wvt_ref[...], u, _NT, preferred_element_type=jnp.float32).astype(bf16)
        for h in range(N_HEADS):
            vt_ref[0, h * V_EXT:h * V_EXT + V_DIM, rows] = vt[h * V_DIM:(h + 1) * V_DIM, :]
            vt_ref[0, h * V_EXT + V_DIM:(h + 1) * V_EXT, rows] = ones_rows

    ubuf_ref[...] = prev


def _projection(x, pos3, g, w32, cw, cb, invf):
    b, s, d = x.shape
    tm = PROJ_TM
    tok = lambda bi, si: (bi, si, 0)
    out_tok = jax.ShapeDtypeStruct((b, s, GROUP), jnp.bfloat16)
    vt_rows = N_HEADS * V_EXT
    return pl.pallas_call(
        functools.partial(_proj_kernel, tm=tm, sub=PROJ_SUB),
        grid=(b, s // tm),
        in_specs=[
            pl.BlockSpec((1, tm, d), tok),
            pl.BlockSpec((1, 1, tm), lambda bi, si: (bi, 0, si)),
            _const_spec((1, d)),
            _const_spec(w32.shape),
            _const_spec(cw.shape),
            _const_spec(cb.shape),
            _const_spec(invf.shape),
        ],
        out_specs=[
            pl.BlockSpec((1, tm, GROUP), tok),
            pl.BlockSpec((1, tm, GROUP), tok),
            pl.BlockSpec((1, tm, GROUP), tok),
            pl.BlockSpec((1, vt_rows, tm), lambda bi, si: (bi, 0, si)),
            pl.BlockSpec((1, tm, GROUP), tok),
        ],
        out_shape=[out_tok, out_tok, out_tok,
                   jax.ShapeDtypeStruct((b, vt_rows, s), jnp.bfloat16), out_tok],
        scratch_shapes=[
            pltpu.VMEM(w32.shape, jnp.bfloat16),
            pltpu.VMEM((GROUP, d), jnp.bfloat16),
            pltpu.VMEM((SUBLANES, GROUP), jnp.float32),
        ],
        compiler_params=pltpu.CompilerParams(
            dimension_semantics=("arbitrary", "arbitrary"), vmem_limit_bytes=VMEM_LIMIT),
        name="proj_conv_rope",
    )(x, pos3, g, w32, cw, cb, invf)


def _attn_kernel(q_ref, qn_ref, k_ref, kn_ref, vt_ref, gate_ref, lam_ref, g_ref, y_ref,
                 s_ref, mt_ref, qz_ref, acc_ref, m_ref, *, tq, tk, lam_init):
    qt = pl.program_id(1)
    first_step = jnp.logical_and(pl.program_id(0) == 0, qt == 0)
    n_chain = 2 * N_HEADS
    heads = [slice(h * LANES, (h + 1) * LANES) for h in range(N_HEADS)]
    lane = lax.broadcasted_iota(jnp.int32, (1, LANES), 1)
    every = slice(0, tq)
    upper = slice(tq // 2, tq)

    def tail_mask(key_tile, cols=every):
        n = cols.stop - cols.start
        kchunk = lax.broadcasted_iota(jnp.int32, (tk, n), 0) // CHUNK + key_tile * (tk // CHUNK)
        qchunk = (lax.broadcasted_iota(jnp.int32, (tk, n), 1) + cols.start) // CHUNK
        return kchunk <= qchunk

    def split_queries(src_ref):
        for h, hs in enumerate(heads):
            qh = src_ref[0, :, hs]
            zero = jnp.zeros_like(qh)
            qz_ref[2 * h] = jnp.where(lane < HEAD_DIM, qh, zero)
            qz_ref[2 * h + 1] = jnp.where(lane >= HEAD_DIM, qh, zero)

    acc_ref[...] = jnp.zeros_like(acc_ref)
    m_ref[...] = jnp.full_like(m_ref, NEG)

    def qk(j, c, slot, mask, cols=every, keys_ref=None):
        if keys_ref is None:
            kh = k_ref[0, pl.ds(pl.multiple_of(j * tk, tk), tk), heads[c // 2]]
        else:
            kh = keys_ref[0, :, heads[c // 2]]
        st = lax.dot_general(kh, qz_ref[c, cols, :], _NT, preferred_element_type=jnp.float32)
        if mask is not None:
            st = jnp.where(mask, st, NEG)
        s_ref[slot, c, :, cols] = st
        mt_ref[slot, c, :, cols] = jnp.max(st, axis=0, keepdims=True)

    def softmax_pv(j, c, slot, cols=every):
        off = pl.multiple_of(j * tk, tk)
        m_old = m_ref[c, :, cols]
        m_new = jnp.maximum(m_old, mt_ref[slot, c, :, cols])
        alpha = jnp.exp2(m_old - m_new)
        p = jnp.exp2(s_ref[slot, c, :, cols] - m_new)
        m_ref[c, :, cols] = m_new
        h = c // 2
        vh = vt_ref[0, h * V_EXT:(h + 1) * V_EXT, pl.ds(off, tk)]
        acc_ref[c, :, cols] = alpha * acc_ref[c, :, cols] + jnp.dot(
            vh, p.astype(jnp.bfloat16), preferred_element_type=jnp.float32)

    lp = lam_ref[...]
    lam = (jnp.exp(jnp.sum(lp[0:1] * lp[1:2], axis=1, keepdims=True))
           - jnp.exp(jnp.sum(lp[2:3] * lp[3:4], axis=1, keepdims=True)) + lam_init)

    def finalize(h):
        hs = heads[h]
        a1, a2 = acc_ref[2 * h], acc_ref[2 * h + 1]
        o = (a1[:V_DIM] * (1.0 / a1[V_DIM:V_DIM + 1])
             - a2[:V_DIM] * (lam / a2[V_DIM:V_DIM + 1]))
        on = o * lax.rsqrt(jnp.mean(o * o, axis=0, keepdims=True) + SUBLN_EPS)
        y = on.T * (g_ref[...] * (1.0 - lam_init)) * gate_ref[0, :, hs].astype(jnp.float32)
        y_ref[0, :, hs] = y.astype(jnp.bfloat16)

    def stage(j, slot, has_next=True, next_mask=None, cols=every, next_cols=every, last=False):
        if last:
            split_queries(qn_ref)
        for c in range(n_chain):
            if has_next:
                qk(j + 1, c, 1 - slot, next_mask, next_cols)
            if last:
                qk(0, c, 1 - slot, None, keys_ref=kn_ref)
            softmax_pv(j, c, slot, cols)
            if last and c % 2 == 1:
                finalize(c // 2)

    @pl.when(first_step)
    def _():
        split_queries(q_ref)
        for c in range(n_chain):
            qk(0, c, 0, None)

    @pl.when(qt == 0)
    def _():
        mask = tail_mask(0)
        for c in range(n_chain):
            st = jnp.where(mask, s_ref[0, c], NEG)
            s_ref[0, c] = st
            mt_ref[0, c] = jnp.max(st, axis=0, keepdims=True)

    n_pairs = jnp.maximum(qt - 1, 0)

    def quad(i, carry):
        for t in range(4):
            stage(4 * i + t, t % 2)
        return carry

    lax.fori_loop(0, n_pairs // 2, quad, 0)

    @pl.when(n_pairs % 2 == 1)
    def _():
        stage(2 * n_pairs - 2, 0)
        stage(2 * n_pairs - 1, 1)

    @pl.when(qt > 0)
    def _():
        stage(2 * qt - 2, 0)
        stage(2 * qt - 1, 1, next_mask=tail_mask(0))

    stage(2 * qt, 0, next_mask=tail_mask(1, upper), next_cols=upper)
    stage(2 * qt + 1, 1, has_next=False, cols=upper, last=True)


def _attention(q, k, vt, gate, lam_params, subln_g, lam_init):
    b, s, w = q.shape
    tq, tk = ATTN_TQ, ATTN_TK
    assert tq == 2 * tk and tk % CHUNK == 0
    n_chain = 2 * N_HEADS
    f32 = jnp.float32
    nq = s // tq
    qtile = lambda bi, qi: (bi, qi, 0)
    next_b = lambda bi, qi: jnp.minimum(bi + (qi + 1) // nq, b - 1)
    return pl.pallas_call(
        functools.partial(_attn_kernel, tq=tq, tk=tk, lam_init=lam_init),
        grid=(b, nq),
        in_specs=[
            pl.BlockSpec((1, tq, w), qtile),
            pl.BlockSpec((1, tq, w), lambda bi, qi: (next_b(bi, qi), (qi + 1) % nq, 0)),
            pl.BlockSpec((1, s, w), lambda bi, qi: (bi, 0, 0)),
            pl.BlockSpec((1, tk, w), lambda bi, qi: (next_b(bi, qi), 0, 0)),
            pl.BlockSpec((1, vt.shape[1], s), lambda bi, qi: (bi, 0, 0)),
            pl.BlockSpec((1, tq, w), qtile),
            _const_spec(lam_params.shape),
            _const_spec(subln_g.shape),
        ],
        out_specs=pl.BlockSpec((1, tq, w), qtile),
        out_shape=jax.ShapeDtypeStruct((b, s, w), jnp.bfloat16),
        scratch_shapes=[
            pltpu.VMEM((2, n_chain, tk, tq), f32),
            pltpu.VMEM((2, n_chain, 1, tq), f32),
            pltpu.VMEM((n_chain, tq, LANES), jnp.bfloat16),
            pltpu.VMEM((n_chain, V_EXT, tq), f32),
            pltpu.VMEM((n_chain, 1, tq), f32),
        ],
        compiler_params=pltpu.CompilerParams(
            dimension_semantics=("arbitrary", "arbitrary"), vmem_limit_bytes=VMEM_LIMIT),
        name="diff_attention",
    )(q, q, k, k, vt, gate, lam_params, subln_g)


def _rms(h, g):
    return h * lax.rsqrt(jnp.mean(h * h, axis=-1, keepdims=True) + EPS) * g


def _out_kernel(x_ref, yc_ref, ya_ref, p_ref, wo_ref, wg_ref, wp_ref, fn_ref, o_ref):
    f32, bf16 = jnp.float32, jnp.bfloat16
    tm = x_ref.shape[0]
    halves = [slice(r0, r0 + OUT_SUB) for r0 in range(0, tm, OUT_SUB)]
    mix = [jnp.dot(yc_ref[rows, :], wo_ref[0:CONV_WIDTH, :], preferred_element_type=f32)
           + jnp.dot(ya_ref[rows, :], wo_ref[CONV_WIDTH:, :], preferred_element_type=f32)
           for rows in halves]
    h, z, pp = [], [], []
    for rows, m in zip(halves, mix):
        h.append(x_ref[rows, :] + m)
        z.append(jnp.dot(h[-1].astype(bf16), wg_ref[...], preferred_element_type=f32))
        pp.append(jnp.dot(p_ref[rows, :].astype(bf16), wp_ref[...], preferred_element_type=f32))
    for rows, hh, zz, pr in zip(halves, h, z, pp):
        inv = lax.rsqrt(jnp.mean(hh * hh, axis=-1, keepdims=True) + EPS)
        o_ref[rows, :] = _rms(hh + jax.nn.sigmoid(zz * inv) * pr, fn_ref[...])


def _output(x2, yc2, ya2, p2, wo_bf, wg_bf, wp_bf, final_norm):
    t, d = x2.shape
    tm = OUT_TM
    row = lambda i: (i, 0)
    return pl.pallas_call(
        _out_kernel,
        grid=(t // tm,),
        in_specs=[
            pl.BlockSpec((tm, d), row),
            pl.BlockSpec((tm, CONV_WIDTH), row),
            pl.BlockSpec((tm, ATTN_WIDTH), row),
            pl.BlockSpec((tm, PLE_DIM), row),
            _const_spec(wo_bf.shape),
            _const_spec(wg_bf.shape),
            _const_spec(wp_bf.shape),
            _const_spec(final_norm.shape),
        ],
        out_specs=pl.BlockSpec((tm, d), row),
        out_shape=jax.ShapeDtypeStruct((t, d), jnp.float32),
        compiler_params=pltpu.CompilerParams(
            dimension_semantics=("parallel",), vmem_limit_bytes=VMEM_LIMIT),
        name="out_ple_norm",
    )(x2, yc2, ya2, p2, wo_bf, wg_bf, wp_bf, final_norm)


def kernel(x, p, positions, norm_mix, w_in, conv_w, conv_b, lambda_q1, lambda_k1, lambda_q2,
           lambda_k2, subln_g, w_out, norm_ple, w_ple_gate, w_ple_proj, final_norm):
    b, s, d = x.shape
    depth = p.shape[0]
    assert depth == 1 and d == D_MODEL and w_in.shape[-1] == 8 * GROUP
    assert s % PROJ_TM == 0 and s % ATTN_TQ == 0 and (b * s) % OUT_TM == 0
    bf16 = jnp.bfloat16
    lam_init = 0.8 - 0.6 * math.exp(-0.3 * 0)

    half = ROT_DIM // 2
    invf = (ROPE_THETA ** (-jnp.arange(half, dtype=jnp.float32) / half))[:, None]
    yconv, q, k, vt, gate = _projection(
        x, positions[:, None, :], norm_mix[0][None, :], w_in[0],
        conv_w[0], conv_b[0][None, :], invf)

    lam_params = jnp.stack([lambda_q1[0], lambda_k1[0], lambda_q2[0], lambda_k2[0]])
    yattn = _attention(q, k, vt, gate, lam_params, subln_g[0][None, :], lam_init)

    out = _output(
        x.reshape(b * s, d), yconv.reshape(b * s, CONV_WIDTH), yattn.reshape(b * s, ATTN_WIDTH),
        p[0].reshape(b * s, PLE_DIM), w_out[0].astype(bf16),
        (norm_ple[0][:, None] * w_ple_gate[0]).astype(bf16),
        w_ple_proj[0].astype(bf16), final_norm[None, :])
    return out.reshape(b, s, d)
```

```python
import functools
import math

import jax
import jax.numpy as jnp
from jax import lax
from jax.experimental import pallas as pl
from jax.experimental.pallas import tpu as pltpu

D_MODEL = 1024
CHUNK = 64
PLE_DIM = 256
CONV_WIDTH = 512
CONV_K = 3
ATTN_WIDTH = 512
N_HEADS = 4
HEAD_DIM = 64
V_DIM = 2 * HEAD_DIM
V_EXT = V_DIM + 16
ROT_DIM = HEAD_DIM // 4
ROPE_THETA = 500000.0
EPS = 1e-6
SUBLN_EPS = 1e-5
GROUP = 512
V_GROUP = 6
LANES = 128
SUBLANES = 8
NEG = -1e30

PROJ_TM = 1024
PROJ_SUB = 256
ATTN_TQ = 512
ATTN_TK = 256
OUT_TM = 1024
OUT_SUB = 256
VMEM_LIMIT = 56 * 1024 * 1024

_NT = (((1,), (1,)), ((), ()))


def _const_spec(shape):
    return pl.BlockSpec(shape, lambda *_: (0,) * len(shape), pipeline_mode=pl.Buffered(1))


def _proj_kernel(x_ref, pos_ref, g_ref, w32_ref, cw_ref, cb_ref, invf_ref,
                 yconv_ref, q_ref, k_ref, vt_ref, gate_ref, w_ref, wvt_ref, ubuf_ref, *, tm, sub):
    si = pl.program_id(1)
    bf16 = jnp.bfloat16

    @pl.when(jnp.logical_and(pl.program_id(0) == 0, si == 0))
    def _():
        for c in range(w32_ref.shape[1] // GROUP):
            cols = slice(c * GROUP, (c + 1) * GROUP)
            w_ref[:, cols] = w32_ref[:, cols].astype(bf16)
        wvt_ref[...] = w32_ref[:, V_GROUP * GROUP:(V_GROUP + 1) * GROUP].T.astype(bf16)

    @pl.when(si == 0)
    def _():
        ubuf_ref[...] = jnp.zeros_like(ubuf_ref)

    prev = ubuf_ref[...]
    row = lax.broadcasted_iota(jnp.int32, (SUBLANES, 1), 0)
    lane = lax.broadcasted_iota(jnp.int32, (1, LANES), 1)
    low = (lane % HEAD_DIM) < (ROT_DIM // 2)
    fill = HEAD_DIM - ROT_DIM
    one, zero = jnp.ones((fill, sub), jnp.float32), jnp.zeros((fill, sub), jnp.float32)
    ones_rows = jnp.ones((V_EXT - V_DIM, sub), bf16)

    for r0 in range(0, tm, sub):
        rows = slice(r0, r0 + sub)
        x = x_ref[0, rows, :]
        ms = jnp.mean(x * x, axis=-1, keepdims=True)
        u = (x * lax.rsqrt(ms + EPS) * g_ref[...]).astype(bf16)

        def proj(c):
            return jnp.dot(u, w_ref[:, c * GROUP:(c + 1) * GROUP], preferred_element_type=jnp.float32)

        uc = proj(2) * proj(0)

        def shifted(k):
            r = pltpu.roll(uc, k, 0)
            head = jnp.where(row < k, pltpu.roll(prev, k, 0), r[0:SUBLANES, :])
            return jnp.concatenate([head, r[SUBLANES:, :]], axis=0)

        conv = (cw_ref[0:1, :] * shifted(2) + cw_ref[1:2, :] * shifted(1)
                + cw_ref[2:3, :] * uc + cb_ref[...])
        prev = uc[sub - SUBLANES:, :]
        cz = proj(3)
        yconv_ref[0, rows, :] = (proj(1) * conv * (cz * jax.nn.sigmoid(cz))).astype(bf16)

        ang = invf_ref[...] * pos_ref[0, :, rows].astype(jnp.float32)
        c8, s8 = jnp.cos(ang), jnp.sin(ang)
        cos = jnp.concatenate([c8, c8, one] * 2, axis=0).T
        sin = jnp.concatenate([-s8, s8, zero] * 2, axis=0).T

        def rope(t, scale):
            outs = []
            for h in range(N_HEADS):
                th = t[:, h * LANES:(h + 1) * LANES]
                partner = jnp.where(low, pltpu.roll(th, LANES - ROT_DIM // 2, 1),
                                    pltpu.roll(th, ROT_DIM // 2, 1))
                r = th * cos + partner * sin
                outs.append(r * scale if scale != 1.0 else r)
            return jnp.concatenate(outs, axis=1)

        q_ref[0, rows, :] = rope(proj(4), HEAD_DIM ** -0.5 * math.log2(math.e)).astype(bf16)
        k_ref[0, rows, :] = rope(proj(5), 1.0).astype(bf16)
        az = proj(7)
        gate_ref[0, rows, :] = (az * jax.nn.sigmoid(az)).astype(bf16)
        vt = lax.dot_general(wvt_ref[...], u, _NT, preferred_element_type=jnp.float32).astype(bf16)
        for h in range(N_HEADS):
            vt_ref[0, h * V_EXT:h * V_EXT + V_DIM, rows] = vt[h * V_DIM:(h + 1) * V_DIM, :]
            vt_ref[0, h * V_EXT + V_DIM:(h + 1) * V_EXT, rows] = ones_rows

    ubuf_ref[...] = prev


def _projection(x, pos3, g, w32, cw, cb, invf):
    b, s, d = x.shape
    tm = PROJ_TM
    tok = lambda bi, si: (bi, si, 0)
    out_tok = jax.ShapeDtypeStruct((b, s, GROUP), jnp.bfloat16)
    vt_rows = N_HEADS * V_EXT
    return pl.pallas_call(
        functools.partial(_proj_kernel, tm=tm, sub=PROJ_SUB),
        grid=(b, s // tm),
        in_specs=[
            pl.BlockSpec((1, tm, d), tok),
            pl.BlockSpec((1, 1, tm), lambda bi, si: (bi, 0, si)),
            _const_spec((1, d)),
            _const_spec(w32.shape),
            _const_spec(cw.shape),
            _const_spec(cb.shape),
            _const_spec(invf.shape),
        ],
        out_specs=[
            pl.BlockSpec((1, tm, GROUP), tok),
            pl.BlockSpec((1, tm, GROUP), tok),
            pl.BlockSpec((1, tm, GROUP), tok),
            pl.BlockSpec((1, vt_rows, tm), lambda bi, si: (bi, 0, si)),
            pl.BlockSpec((1, tm, GROUP), tok),
        ],
        out_shape=[out_tok, out_tok, out_tok,
                   jax.ShapeDtypeStruct((b, vt_rows, s), jnp.bfloat16), out_tok],
        scratch_shapes=[
            pltpu.VMEM(w32.shape, jnp.bfloat16),
            pltpu.VMEM((GROUP, d), jnp.bfloat16),
            pltpu.VMEM((SUBLANES, GROUP), jnp.float32),
        ],
        compiler_params=pltpu.CompilerParams(
            dimension_semantics=("arbitrary", "arbitrary"), vmem_limit_bytes=VMEM_LIMIT),
        name="proj_conv_rope",
    )(x, pos3, g, w32, cw, cb, invf)


def _attn_kernel(q_ref, qn_ref, k_ref, kn_ref, vt_ref, gate_ref, lam_ref, g_ref, y_ref,
                 s_ref, mt_ref, qz_ref, acc_ref, m_ref, *, tq, tk, lam_init):
    qt = pl.program_id(1)
    first_step = jnp.logical_and(pl.program_id(0) == 0, qt == 0)
    n_chain = 2 * N_HEADS
    heads = [slice(h * LANES, (h + 1) * LANES) for h in range(N_HEADS)]
    lane = lax.broadcasted_iota(jnp.int32, (1, LANES), 1)
    every = slice(0, tq)
    upper = slice(tq // 2, tq)

    def tail_mask(key_tile, cols=every):
        n = cols.stop - cols.start
        kchunk = lax.broadcasted_iota(jnp.int32, (tk, n), 0) // CHUNK + key_tile * (tk // CHUNK)
        qchunk = (lax.broadcasted_iota(jnp.int32, (tk, n), 1) + cols.start) // CHUNK
        return kchunk <= qchunk

    def split_queries(src_ref):
        for h, hs in enumerate(heads):
            qh = src_ref[0, :, hs]
            zero = jnp.zeros_like(qh)
            qz_ref[2 * h] = jnp.where(lane < HEAD_DIM, qh, zero)
            qz_ref[2 * h + 1] = jnp.where(lane >= HEAD_DIM, qh, zero)

    acc_ref[...] = jnp.zeros_like(acc_ref)
    m_ref[...] = jnp.full_like(m_ref, NEG)

    def qk(j, c, slot, mask, cols=every, keys_ref=None):
        if keys_ref is None:
            kh = k_ref[0, pl.ds(pl.multiple_of(j * tk, tk), tk), heads[c // 2]]
        else:
            kh = keys_ref[0, :, heads[c // 2]]
        st = lax.dot_general(kh, qz_ref[c, cols, :], _NT, preferred_element_type=jnp.float32)
        if mask is not None:
            st = jnp.where(mask, st, NEG)
        s_ref[slot, c, :, cols] = st
        mt_ref[slot, c, :, cols] = jnp.max(st, axis=0, keepdims=True)

    def softmax_pv(j, c, slot, cols=every):
        off = pl.multiple_of(j * tk, tk)
        m_old = m_ref[c, :, cols]
        m_new = jnp.maximum(m_old, mt_ref[slot, c, :, cols])
        alpha = jnp.exp2(m_old - m_new)
        p = jnp.exp2(s_ref[slot, c, :, cols] - m_new)
        m_ref[c, :, cols] = m_new
        h = c // 2
        vh = vt_ref[0, h * V_EXT:(h + 1) * V_EXT, pl.ds(off, tk)]
        acc_ref[c, :, cols] = alpha * acc_ref[c, :, cols] + jnp.dot(
            vh, p.astype(jnp.bfloat16), preferred_element_type=jnp.float32)

    lp = lam_ref[...]
    lam = (jnp.exp(jnp.sum(lp[0:1] * lp[1:2], axis=1, keepdims=True))
           - jnp.exp(jnp.sum(lp[2:3] * lp[3:4], axis=1, keepdims=True)) + lam_init)

    def finalize(h):
        hs = heads[h]
        a1, a2 = acc_ref[2 * h], acc_ref[2 * h + 1]
        o = (a1[:V_DIM] * (1.0 / a1[V_DIM:V_DIM + 1])
             - a2[:V_DIM] * (lam / a2[V_DIM:V_DIM + 1]))
        on = o * lax.rsqrt(jnp.mean(o * o, axis=0, keepdims=True) + SUBLN_EPS)
        y = on.T * (g_ref[...] * (1.0 - lam_init)) * gate_ref[0, :, hs].astype(jnp.float32)
        y_ref[0, :, hs] = y.astype(jnp.bfloat16)

    def stage(j, slot, has_next=True, next_mask=None, cols=every, next_cols=every, last=False):
        if last:
            split_queries(qn_ref)
        for c in range(n_chain):
            if has_next:
                qk(j + 1, c, 1 - slot, next_mask, next_cols)
            if last:
                qk(0, c, 1 - slot, None, keys_ref=kn_ref)
            softmax_pv(j, c, slot, cols)
            if last and c % 2 == 1:
                finalize(c // 2)

    @pl.when(first_step)
    def _():
        split_queries(q_ref)
        for c in range(n_chain):
            qk(0, c, 0, None)

    @pl.when(qt == 0)
    def _():
        mask = tail_mask(0)
        for c in range(n_chain):
            st = jnp.where(mask, s_ref[0, c], NEG)
            s_ref[0, c] = st
            mt_ref[0, c] = jnp.max(st, axis=0, keepdims=True)

    n_pairs = jnp.maximum(qt - 1, 0)

    def quad(i, carry):
        for t in range(4):
            stage(4 * i + t, t % 2)
        return carry

    lax.fori_loop(0, n_pairs // 2, quad, 0)

    @pl.when(n_pairs % 2 == 1)
    def _():
        stage(2 * n_pairs - 2, 0)
        stage(2 * n_pairs - 1, 1)

    def diagonal_stages():
        stage(2 * qt, 0, next_mask=tail_mask(1, upper), next_cols=upper)
        stage(2 * qt + 1, 1, has_next=False, cols=upper, last=True)

    @pl.when(qt > 0)
    def _():
        stage(2 * qt - 2, 0)
        stage(2 * qt - 1, 1, next_mask=tail_mask(0))
        diagonal_stages()

    @pl.when(qt == 0)
    def _():
        diagonal_stages()


def _attention(q, k, vt, gate, lam_params, subln_g, lam_init):
    b, s, w = q.shape
    tq, tk = ATTN_TQ, ATTN_TK
    assert tq == 2 * tk and tk % CHUNK == 0
    n_chain = 2 * N_HEADS
    f32 = jnp.float32
    nq = s // tq
    qtile = lambda bi, qi: (bi, qi, 0)
    next_b = lambda bi, qi: jnp.minimum(bi + (qi + 1) // nq, b - 1)
    return pl.pallas_call(
        functools.partial(_attn_kernel, tq=tq, tk=tk, lam_init=lam_init),
        grid=(b, nq),
        in_specs=[
            pl.BlockSpec((1, tq, w), qtile),
            pl.BlockSpec((1, tq, w), lambda bi, qi: (next_b(bi, qi), (qi + 1) % nq, 0)),
            pl.BlockSpec((1, s, w), lambda bi, qi: (bi, 0, 0)),
            pl.BlockSpec((1, tk, w), lambda bi, qi: (next_b(bi, qi), 0, 0)),
            pl.BlockSpec((1, vt.shape[1], s), lambda bi, qi: (bi, 0, 0)),
            pl.BlockSpec((1, tq, w), qtile),
            _const_spec(lam_params.shape),
            _const_spec(subln_g.shape),
        ],
        out_specs=pl.BlockSpec((1, tq, w), qtile),
        out_shape=jax.ShapeDtypeStruct((b, s, w), jnp.bfloat16),
        scratch_shapes=[
            pltpu.VMEM((2, n_chain, tk, tq), f32),
            pltpu.VMEM((2, n_chain, 1, tq), f32),
            pltpu.VMEM((n_chain, tq, LANES), jnp.bfloat16),
            pltpu.VMEM((n_chain, V_EXT, tq), f32),
            pltpu.VMEM((n_chain, 1, tq), f32),
        ],
        compiler_params=pltpu.CompilerParams(
            dimension_semantics=("arbitrary", "arbitrary"), vmem_limit_bytes=VMEM_LIMIT),
        name="diff_attention",
    )(q, q, k, k, vt, gate, lam_params, subln_g)


def _rms(h, g):
    return h * lax.rsqrt(jnp.mean(h * h, axis=-1, keepdims=True) + EPS) * g


def _out_kernel(x_ref, yc_ref, ya_ref, p_ref, wo_ref, wg_ref, wp_ref, fn_ref, o_ref):
    f32, bf16 = jnp.float32, jnp.bfloat16
    tm = x_ref.shape[0]
    halves = [slice(r0, r0 + OUT_SUB) for r0 in range(0, tm, OUT_SUB)]
    mix = [jnp.dot(yc_ref[rows, :], wo_ref[0:CONV_WIDTH, :], preferred_element_type=f32)
           + jnp.dot(ya_ref[rows, :], wo_ref[CONV_WIDTH:, :], preferred_element_type=f32)
           for rows in halves]
    h, z, pp = [], [], []
    for rows, m in zip(halves, mix):
        h.append(x_ref[rows, :] + m)
        z.append(jnp.dot(h[-1].astype(bf16), wg_ref[...], preferred_element_type=f32))
        pp.append(jnp.dot(p_ref[rows, :].astype(bf16), wp_ref[...], preferred_element_type=f32))
    for rows, hh, zz, pr in zip(halves, h, z, pp):
        inv = lax.rsqrt(jnp.mean(hh * hh, axis=-1, keepdims=True) + EPS)
        o_ref[rows, :] = _rms(hh + jax.nn.sigmoid(zz * inv) * pr, fn_ref[...])


def _output(x2, yc2, ya2, p2, wo_bf, wg_bf, wp_bf, final_norm):
    t, d = x2.shape
    tm = OUT_TM
    row = lambda i: (i, 0)
    return pl.pallas_call(
        _out_kernel,
        grid=(t // tm,),
        in_specs=[
            pl.BlockSpec((tm, d), row),
            pl.BlockSpec((tm, CONV_WIDTH), row),
            pl.BlockSpec((tm, ATTN_WIDTH), row),
            pl.BlockSpec((tm, PLE_DIM), row),
            _const_spec(wo_bf.shape),
            _const_spec(wg_bf.shape),
            _const_spec(wp_bf.shape),
            _const_spec(final_norm.shape),
        ],
        out_specs=pl.BlockSpec((tm, d), row),
        out_shape=jax.ShapeDtypeStruct((t, d), jnp.float32),
        compiler_params=pltpu.CompilerParams(
            dimension_semantics=("parallel",), vmem_limit_bytes=VMEM_LIMIT),
        name="out_ple_norm",
    )(x2, yc2, ya2, p2, wo_bf, wg_bf, wp_bf, final_norm)


def kernel(x, p, positions, norm_mix, w_in, conv_w, conv_b, lambda_q1, lambda_k1, lambda_q2,
           lambda_k2, subln_g, w_out, norm_ple, w_ple_gate, w_ple_proj, final_norm):
    b, s, d = x.shape
    depth = p.shape[0]
    assert depth == 1 and d == D_MODEL and w_in.shape[-1] == 8 * GROUP
    assert s % PROJ_TM == 0 and s % ATTN_TQ == 0 and (b * s) % OUT_TM == 0
    bf16 = jnp.bfloat16
    lam_init = 0.8 - 0.6 * math.exp(-0.3 * 0)

    half = ROT_DIM // 2
    invf = (ROPE_THETA ** (-jnp.arange(half, dtype=jnp.float32) / half))[:, None]
    yconv, q, k, vt, gate = _projection(
        x, positions[:, None, :], norm_mix[0][None, :], w_in[0],
        conv_w[0], conv_b[0][None, :], invf)

    lam_params = jnp.stack([lambda_q1[0], lambda_k1[0], lambda_q2[0], lambda_k2[0]])
    yattn = _attention(q, k, vt, gate, lam_params, subln_g[0][None, :], lam_init)

    out = _output(
        x.reshape(b * s, d), yconv.reshape(b * s, CONV_WIDTH), yattn.reshape(b * s, ATTN_WIDTH),
        p[0].reshape(b * s, PLE_DIM), w_out[0].astype(bf16),
        (norm_ple[0][:, None] * w_ple_gate[0]).astype(bf16),
        w_ple_proj[0].astype(bf16), final_norm[None, :])
    return out.reshape(b, s, d)
```

```python
import functools
import math

import jax
import jax.numpy as jnp
from jax import lax
from jax.experimental import pallas as pl
from jax.experimental.pallas import tpu as pltpu

D_MODEL = 1024
CHUNK = 64
PLE_DIM = 256
CONV_WIDTH = 512
CONV_K = 3
ATTN_WIDTH = 512
N_HEADS = 4
HEAD_DIM = 64
V_DIM = 2 * HEAD_DIM
V_EXT = V_DIM + 16
ROT_DIM = HEAD_DIM // 4
ROPE_THETA = 500000.0
EPS = 1e-6
SUBLN_EPS = 1e-5
GROUP = 512
V_GROUP = 6
LANES = 128
SUBLANES = 8
NEG = -1e30

PROJ_TM = 1024
PROJ_SUB = 256
ATTN_TQ = 512
ATTN_TK = 256
OUT_TM = 1024
OUT_SUB = 256
VMEM_LIMIT = 56 * 1024 * 1024

_NT = (((1,), (1,)), ((), ()))


def _const_spec(shape):
    return pl.BlockSpec(shape, lambda *_: (0,) * len(shape), pipeline_mode=pl.Buffered(1))


def _proj_kernel(x_ref, pos_ref, g_ref, w32_ref, cw_ref, cb_ref, invf_ref,
                 yconv_ref, q_ref, k_ref, vt_ref, gate_ref, w_ref, wvt_ref, ubuf_ref, *, tm, sub):
    si = pl.program_id(1)
    bf16 = jnp.bfloat16

    @pl.when(jnp.logical_and(pl.program_id(0) == 0, si == 0))
    def _():
        for c in range(w32_ref.shape[1] // GROUP):
            cols = slice(c * GROUP, (c + 1) * GROUP)
            w_ref[:, cols] = w32_ref[:, cols].astype(bf16)
        wvt_ref[...] = w32_ref[:, V_GROUP * GROUP:(V_GROUP + 1) * GROUP].T.astype(bf16)

    @pl.when(si == 0)
    def _():
        ubuf_ref[...] = jnp.zeros_like(ubuf_ref)

    prev = ubuf_ref[...]
    row = lax.broadcasted_iota(jnp.int32, (SUBLANES, 1), 0)
    lane = lax.broadcasted_iota(jnp.int32, (1, LANES), 1)
    low = (lane % HEAD_DIM) < (ROT_DIM // 2)
    fill = HEAD_DIM - ROT_DIM
    one, zero = jnp.ones((fill, sub), jnp.float32), jnp.zeros((fill, sub), jnp.float32)
    ones_rows = jnp.ones((V_EXT - V_DIM, sub), bf16)

    for r0 in range(0, tm, sub):
        rows = slice(r0, r0 + sub)
        x = x_ref[0, rows, :]
        ms = jnp.mean(x * x, axis=-1, keepdims=True)
        u = (x * lax.rsqrt(ms + EPS) * g_ref[...]).astype(bf16)

        def proj(c):
            return jnp.dot(u, w_ref[:, c * GROUP:(c + 1) * GROUP], preferred_element_type=jnp.float32)

        uc = proj(2) * proj(0)

        def shifted(k):
            r = pltpu.roll(uc, k, 0)
            head = jnp.where(row < k, pltpu.roll(prev, k, 0), r[0:SUBLANES, :])
            return jnp.concatenate([head, r[SUBLANES:, :]], axis=0)

        conv = (cw_ref[0:1, :] * shifted(2) + cw_ref[1:2, :] * shifted(1)
                + cw_ref[2:3, :] * uc + cb_ref[...])
        prev = uc[sub - SUBLANES:, :]
        cz = proj(3)
        yconv_ref[0, rows, :] = (proj(1) * conv * (cz * jax.nn.sigmoid(cz))).astype(bf16)

        ang = invf_ref[...] * pos_ref[0, :, rows].astype(jnp.float32)
        c8, s8 = jnp.cos(ang), jnp.sin(ang)
        cos = jnp.concatenate([c8, c8, one] * 2, axis=0).T
        sin = jnp.concatenate([-s8, s8, zero] * 2, axis=0).T

        def rope(t, scale):
            outs = []
            for h in range(N_HEADS):
                th = t[:, h * LANES:(h + 1) * LANES]
                partner = jnp.where(low, pltpu.roll(th, LANES - ROT_DIM // 2, 1),
                                    pltpu.roll(th, ROT_DIM // 2, 1))
                r = th * cos + partner * sin
                outs.append(r * scale if scale != 1.0 else r)
            return jnp.concatenate(outs, axis=1)

        q_ref[0, rows, :] = rope(proj(4), HEAD_DIM ** -0.5 * math.log2(math.e)).astype(bf16)
        k_ref[0, rows, :] = rope(proj(5), 1.0).astype(bf16)
        az = proj(7)
        gate_ref[0, rows, :] = (az * jax.nn.sigmoid(az)).astype(bf16)
        vt = lax.dot_general(wvt_ref[...], u, _NT, preferred_element_type=jnp.float32).astype(bf16)
        for h in range(N_HEADS):
            vt_ref[0, h * V_EXT:h * V_EXT + V_DIM, rows] = vt[h * V_DIM:(h + 1) * V_DIM, :]
            vt_ref[0, h * V_EXT + V_DIM:(h + 1) * V_EXT, rows] = ones_rows

    ubuf_ref[...] = prev


def _projection(x, pos3, g, w32, cw, cb, invf):
    b, s, d = x.shape
    tm = PROJ_TM
    tok = lambda bi, si: (bi, si, 0)
    out_tok = jax.ShapeDtypeStruct((b, s, GROUP), jnp.bfloat16)
    vt_rows = N_HEADS * V_EXT
    return pl.pallas_call(
        functools.partial(_proj_kernel, tm=tm, sub=PROJ_SUB),
        grid=(b, s // tm),
        in_specs=[
            pl.BlockSpec((1, tm, d), tok),
            pl.BlockSpec((1, 1, tm), lambda bi, si: (bi, 0, si)),
            _const_spec((1, d)),
            _const_spec(w32.shape),
            _const_spec(cw.shape),
            _const_spec(cb.shape),
            _const_spec(invf.shape),
        ],
        out_specs=[
            pl.BlockSpec((1, tm, GROUP), tok),
            pl.BlockSpec((1, tm, GROUP), tok),
            pl.BlockSpec((1, tm, GROUP), tok),
            pl.BlockSpec((1, vt_rows, tm), lambda bi, si: (bi, 0, si)),
            pl.BlockSpec((1, tm, GROUP), tok),
        ],
        out_shape=[out_tok, out_tok, out_tok,
                   jax.ShapeDtypeStruct((b, vt_rows, s), jnp.bfloat16), out_tok],
        scratch_shapes=[
            pltpu.VMEM(w32.shape, jnp.bfloat16),
            pltpu.VMEM((GROUP, d), jnp.bfloat16),
            pltpu.VMEM((SUBLANES, GROUP), jnp.float32),
        ],
        compiler_params=pltpu.CompilerParams(
            dimension_semantics=("arbitrary", "arbitrary"), vmem_limit_bytes=VMEM_LIMIT),
        name="proj_conv_rope",
    )(x, pos3, g, w32, cw, cb, invf)


def _attn_kernel(q_ref, qn_ref, k_ref, kn_ref, vt_ref, gate_ref, lam_ref, g_ref, y_ref,
                 s_ref, mt_ref, qz_ref, acc_ref, m_ref, *, tq, tk, lam_init):
    qt = pl.program_id(1)
    first_step = jnp.logical_and(pl.program_id(0) == 0, qt == 0)
    n_chain = 2 * N_HEADS
    heads = [slice(h * LANES, (h + 1) * LANES) for h in range(N_HEADS)]
    lane = lax.broadcasted_iota(jnp.int32, (1, LANES), 1)
    every = slice(0, tq)
    upper = slice(tq // 2, tq)

    def tail_mask(key_tile, cols=every):
        n = cols.stop - cols.start
        kchunk = lax.broadcasted_iota(jnp.int32, (tk, n), 0) // CHUNK + key_tile * (tk // CHUNK)
        qchunk = (lax.broadcasted_iota(jnp.int32, (tk, n), 1) + cols.start) // CHUNK
        return kchunk <= qchunk

    def split_queries(src_ref):
        for h, hs in enumerate(heads):
            qh = src_ref[0, :, hs]
            zero = jnp.zeros_like(qh)
            qz_ref[2 * h] = jnp.where(lane < HEAD_DIM, qh, zero)
            qz_ref[2 * h + 1] = jnp.where(lane >= HEAD_DIM, qh, zero)

    acc_ref[...] = jnp.zeros_like(acc_ref)
    m_ref[...] = jnp.full_like(m_ref, NEG)

    def qk(j, c, slot, mask, cols=every, keys_ref=None):
        if keys_ref is None:
            kh = k_ref[0, pl.ds(pl.multiple_of(j * tk, tk), tk), heads[c // 2]]
        else:
            kh = keys_ref[0, :, heads[c // 2]]
        st = lax.dot_general(kh, qz_ref[c, cols, :], _NT, preferred_element_type=jnp.float32)
        if mask is not None:
            st = jnp.where(mask, st, NEG)
        s_ref[slot, c, :, cols] = st
        mt_ref[slot, c, :, cols] = jnp.max(st, axis=0, keepdims=True)

    def softmax_pv(j, c, slot, cols=every):
        off = pl.multiple_of(j * tk, tk)
        m_old = m_ref[c, :, cols]
        m_new = jnp.maximum(m_old, mt_ref[slot, c, :, cols])
        alpha = jnp.exp2(m_old - m_new)
        p = jnp.exp2(s_ref[slot, c, :, cols] - m_new)
        m_ref[c, :, cols] = m_new
        h = c // 2
        vh = vt_ref[0, h * V_EXT:(h + 1) * V_EXT, pl.ds(off, tk)]
        acc_ref[c, :, cols] = alpha * acc_ref[c, :, cols] + jnp.dot(
            vh, p.astype(jnp.bfloat16), preferred_element_type=jnp.float32)

    lp = lam_ref[...]
    lam = (jnp.exp(jnp.sum(lp[0:1] * lp[1:2], axis=1, keepdims=True))
           - jnp.exp(jnp.sum(lp[2:3] * lp[3:4], axis=1, keepdims=True)) + lam_init)

    def finalize(h):
        hs = heads[h]
        a1, a2 = acc_ref[2 * h], acc_ref[2 * h + 1]
        o = (a1[:V_DIM] * (1.0 / a1[V_DIM:V_DIM + 1])
             - a2[:V_DIM] * (lam / a2[V_DIM:V_DIM + 1]))
        on = o * lax.rsqrt(jnp.mean(o * o, axis=0, keepdims=True) + SUBLN_EPS)
        y = on.T * (g_ref[...] * (1.0 - lam_init)) * gate_ref[0, :, hs].astype(jnp.float32)
        y_ref[0, :, hs] = y.astype(jnp.bfloat16)

    def stage(j, slot, has_next=True, next_mask=None, cols=every, next_cols=every, last=False):
        if last:
            split_queries(qn_ref)
        for c in range(n_chain):
            if has_next:
                qk(j + 1, c, 1 - slot, next_mask, next_cols)
            if last:
                qk(0, c, 1 - slot, None, keys_ref=kn_ref)
            softmax_pv(j, c, slot, cols)
            if last and c % 2 == 1:
                finalize(c // 2)

    @pl.when(first_step)
    def _():
        split_queries(q_ref)
        for c in range(n_chain):
            qk(0, c, 0, None)

    @pl.when(qt == 0)
    def _():
        mask = tail_mask(0)
        for c in range(n_chain):
            st = jnp.where(mask, s_ref[0, c], NEG)
            s_ref[0, c] = st
            mt_ref[0, c] = jnp.max(st, axis=0, keepdims=True)

    n_pairs = jnp.maximum(qt - 1, 0)

    def run(first_tile, n_stages):
        for t in range(n_stages):
            stage(first_tile + t, t % 2)

    n_oct = n_pairs // 4
    lax.fori_loop(0, n_oct, lambda i, c: (run(8 * i, 8), c)[1], 0)

    @pl.when((n_pairs % 4) // 2 == 1)
    def _():
        run(8 * n_oct, 4)

    def diagonal_stages():
        stage(2 * qt, 0, next_mask=tail_mask(1, upper), next_cols=upper)
        stage(2 * qt + 1, 1, has_next=False, cols=upper, last=True)

    def final_stages(n_before):
        for t in range(n_before):
            j = 2 * qt - n_before + t
            stage(j, t % 2, next_mask=tail_mask(0) if t == n_before - 1 else None)
        diagonal_stages()

    odd = n_pairs % 2 == 1

    @pl.when(jnp.logical_and(qt > 0, odd))
    def _():
        final_stages(4)

    @pl.when(jnp.logical_and(qt > 0, jnp.logical_not(odd)))
    def _():
        final_stages(2)

    @pl.when(qt == 0)
    def _():
        final_stages(0)


def _attention(q, k, vt, gate, lam_params, subln_g, lam_init):
    b, s, w = q.shape
    tq, tk = ATTN_TQ, ATTN_TK
    assert tq == 2 * tk and tk % CHUNK == 0
    n_chain = 2 * N_HEADS
    f32 = jnp.float32
    nq = s // tq
    qtile = lambda bi, qi: (bi, qi, 0)
    next_b = lambda bi, qi: jnp.minimum(bi + (qi + 1) // nq, b - 1)
    return pl.pallas_call(
        functools.partial(_attn_kernel, tq=tq, tk=tk, lam_init=lam_init),
        grid=(b, nq),
        in_specs=[
            pl.BlockSpec((1, tq, w), qtile),
            pl.BlockSpec((1, tq, w), lambda bi, qi: (next_b(bi, qi), (qi + 1) % nq, 0)),
            pl.BlockSpec((1, s, w), lambda bi, qi: (bi, 0, 0)),
            pl.BlockSpec((1, tk, w), lambda bi, qi: (next_b(bi, qi), 0, 0)),
            pl.BlockSpec((1, vt.shape[1], s), lambda bi, qi: (bi, 0, 0)),
            pl.BlockSpec((1, tq, w), qtile),
            _const_spec(lam_params.shape),
            _const_spec(subln_g.shape),
        ],
        out_specs=pl.BlockSpec((1, tq, w), qtile),
        out_shape=jax.ShapeDtypeStruct((b, s, w), jnp.bfloat16),
        scratch_shapes=[
            pltpu.VMEM((2, n_chain, tk, tq), f32),
            pltpu.VMEM((2, n_chain, 1, tq), f32),
            pltpu.VMEM((n_chain, tq, LANES), jnp.bfloat16),
            pltpu.VMEM((n_chain, V_EXT, tq), f32),
            pltpu.VMEM((n_chain, 1, tq), f32),
        ],
        compiler_params=pltpu.CompilerParams(
            dimension_semantics=("arbitrary", "arbitrary"), vmem_limit_bytes=VMEM_LIMIT),
        name="diff_attention",
    )(q, q, k, k, vt, gate, lam_params, subln_g)


def _rms(h, g):
    return h * lax.rsqrt(jnp.mean(h * h, axis=-1, keepdims=True) + EPS) * g


def _out_kernel(x_ref, yc_ref, ya_ref, p_ref, wo_ref, wg_ref, wp_ref, fn_ref, o_ref):
    f32, bf16 = jnp.float32, jnp.bfloat16
    tm = x_ref.shape[0]
    halves = [slice(r0, r0 + OUT_SUB) for r0 in range(0, tm, OUT_SUB)]
    mix = [jnp.dot(yc_ref[rows, :], wo_ref[0:CONV_WIDTH, :], preferred_element_type=f32)
           + jnp.dot(ya_ref[rows, :], wo_ref[CONV_WIDTH:, :], preferred_element_type=f32)
           for rows in halves]
    h, z, pp = [], [], []
    for rows, m in zip(halves, mix):
        h.append(x_ref[rows, :] + m)
        z.append(jnp.dot(h[-1].astype(bf16), wg_ref[...], preferred_element_type=f32))
        pp.append(jnp.dot(p_ref[rows, :].astype(bf16), wp_ref[...], preferred_element_type=f32))
    for rows, hh, zz, pr in zip(halves, h, z, pp):
        inv = lax.rsqrt(jnp.mean(hh * hh, axis=-1, keepdims=True) + EPS)
        o_ref[rows, :] = _rms(hh + jax.nn.sigmoid(zz * inv) * pr, fn_ref[...])


def _output(x2, yc2, ya2, p2, wo_bf, wg_bf, wp_bf, final_norm):
    t, d = x2.shape
    tm = OUT_TM
    row = lambda i: (i, 0)
    return pl.pallas_call(
        _out_kernel,
        grid=(t // tm,),
        in_specs=[
            pl.BlockSpec((tm, d), row),
            pl.BlockSpec((tm, CONV_WIDTH), row),
            pl.BlockSpec((tm, ATTN_WIDTH), row),
            pl.BlockSpec((tm, PLE_DIM), row),
            _const_spec(wo_bf.shape),
            _const_spec(wg_bf.shape),
            _const_spec(wp_bf.shape),
            _const_spec(final_norm.shape),
        ],
        out_specs=pl.BlockSpec((tm, d), row),
        out_shape=jax.ShapeDtypeStruct((t, d), jnp.float32),
        compiler_params=pltpu.CompilerParams(
            dimension_semantics=("parallel",), vmem_limit_bytes=VMEM_LIMIT),
        name="out_ple_norm",
    )(x2, yc2, ya2, p2, wo_bf, wg_bf, wp_bf, final_norm)


def kernel(x, p, positions, norm_mix, w_in, conv_w, conv_b, lambda_q1, lambda_k1, lambda_q2,
           lambda_k2, subln_g, w_out, norm_ple, w_ple_gate, w_ple_proj, final_norm):
    b, s, d = x.shape
    depth = p.shape[0]
    assert depth == 1 and d == D_MODEL and w_in.shape[-1] == 8 * GROUP
    assert s % PROJ_TM == 0 and s % ATTN_TQ == 0 and (b * s) % OUT_TM == 0
    bf16 = jnp.bfloat16
    lam_init = 0.8 - 0.6 * math.exp(-0.3 * 0)

    half = ROT_DIM // 2
    invf = (ROPE_THETA ** (-jnp.arange(half, dtype=jnp.float32) / half))[:, None]
    yconv, q, k, vt, gate = _projection(
        x, positions[:, None, :], norm_mix[0][None, :], w_in[0],
        conv_w[0], conv_b[0][None, :], invf)

    lam_params = jnp.stack([lambda_q1[0], lambda_k1[0], lambda_q2[0], lambda_k2[0]])
    yattn = _attention(q, k, vt, gate, lam_params, subln_g[0][None, :], lam_init)

    out = _output(
        x.reshape(b * s, d), yconv.reshape(b * s, CONV_WIDTH), yattn.reshape(b * s, ATTN_WIDTH),
        p[0].reshape(b * s, PLE_DIM), w_out[0].astype(bf16),
        (norm_ple[0][:, None] * w_ple_gate[0]).astype(bf16),
        w_ple_proj[0].astype(bf16), final_norm[None, :])
    return out.reshape(b, s, d)
```

```python
import functools
import math

import jax
import jax.numpy as jnp
from jax import lax
from jax.experimental import pallas as pl
from jax.experimental.pallas import tpu as pltpu

D_MODEL = 1024
CHUNK = 64
PLE_DIM = 256
CONV_WIDTH = 512
CONV_K = 3
ATTN_WIDTH = 512
N_HEADS = 4
HEAD_DIM = 64
V_DIM = 2 * HEAD_DIM
V_EXT = V_DIM + 16
ROT_DIM = HEAD_DIM // 4
ROPE_THETA = 500000.0
EPS = 1e-6
SUBLN_EPS = 1e-5
GROUP = 512
V_GROUP = 6
LANES = 128
SUBLANES = 8
NEG = -1e30

PROJ_TM = 1024
PROJ_SUB = 256
ATTN_TQ = 512
ATTN_TK = 256
OUT_TM = 1024
OUT_SUB = 256
VMEM_LIMIT = 56 * 1024 * 1024

_NT = (((1,), (1,)), ((), ()))


def _const_spec(shape):
    return pl.BlockSpec(shape, lambda *_: (0,) * len(shape), pipeline_mode=pl.Buffered(1))


def _proj_kernel(x_ref, pos_ref, g_ref, w32_ref, cw_ref, cb_ref, invf_ref,
                 yconv_ref, q_ref, k_ref, vt_ref, gate_ref, w_ref, wvt_ref, ubuf_ref, *, tm, sub):
    si = pl.program_id(1)
    bf16 = jnp.bfloat16

    @pl.when(jnp.logical_and(pl.program_id(0) == 0, si == 0))
    def _():
        for c in range(w32_ref.shape[1] // GROUP):
            cols = slice(c * GROUP, (c + 1) * GROUP)
            w_ref[:, cols] = w32_ref[:, cols].astype(bf16)
        wvt_ref[...] = w32_ref[:, V_GROUP * GROUP:(V_GROUP + 1) * GROUP].T.astype(bf16)

    @pl.when(si == 0)
    def _():
        ubuf_ref[...] = jnp.zeros_like(ubuf_ref)

    prev = ubuf_ref[...]
    row = lax.broadcasted_iota(jnp.int32, (SUBLANES, 1), 0)
    lane = lax.broadcasted_iota(jnp.int32, (1, LANES), 1)
    low = (lane % HEAD_DIM) < (ROT_DIM // 2)
    fill = HEAD_DIM - ROT_DIM
    one, zero = jnp.ones((fill, sub), jnp.float32), jnp.zeros((fill, sub), jnp.float32)
    ones_rows = jnp.ones((V_EXT - V_DIM, sub), bf16)

    for r0 in range(0, tm, sub):
        rows = slice(r0, r0 + sub)
        x = x_ref[0, rows, :]
        ms = jnp.mean(x * x, axis=-1, keepdims=True)
        u = (x * lax.rsqrt(ms + EPS) * g_ref[...]).astype(bf16)

        groups = [None if c == V_GROUP else
                  jnp.dot(u, w_ref[:, c * GROUP:(c + 1) * GROUP], preferred_element_type=jnp.float32)
                  for c in range(w_ref.shape[1] // GROUP)]
        vt = lax.dot_general(wvt_ref[...], u, _NT, preferred_element_type=jnp.float32).astype(bf16)
        proj = groups.__getitem__

        uc = proj(2) * proj(0)

        def shifted(k):
            r = pltpu.roll(uc, k, 0)
            head = jnp.where(row < k, pltpu.roll(prev, k, 0), r[0:SUBLANES, :])
            return jnp.concatenate([head, r[SUBLANES:, :]], axis=0)

        conv = (cw_ref[0:1, :] * shifted(2) + cw_ref[1:2, :] * shifted(1)
                + cw_ref[2:3, :] * uc + cb_ref[...])
        prev = uc[sub - SUBLANES:, :]
        cz = proj(3)
        yconv_ref[0, rows, :] = (proj(1) * conv * (cz * jax.nn.sigmoid(cz))).astype(bf16)

        ang = invf_ref[...] * pos_ref[0, :, rows].astype(jnp.float32)
        c8, s8 = jnp.cos(ang), jnp.sin(ang)
        cos = jnp.concatenate([c8, c8, one] * 2, axis=0).T
        sin = jnp.concatenate([-s8, s8, zero] * 2, axis=0).T

        def rope(t, scale):
            outs = []
            for h in range(N_HEADS):
                th = t[:, h * LANES:(h + 1) * LANES]
                partner = jnp.where(low, pltpu.roll(th, LANES - ROT_DIM // 2, 1),
                                    pltpu.roll(th, ROT_DIM // 2, 1))
                r = th * cos + partner * sin
                outs.append(r * scale if scale != 1.0 else r)
            return jnp.concatenate(outs, axis=1)

        q_ref[0, rows, :] = rope(proj(4), HEAD_DIM ** -0.5 * math.log2(math.e)).astype(bf16)
        k_ref[0, rows, :] = rope(proj(5), 1.0).astype(bf16)
        az = proj(7)
        gate_ref[0, rows, :] = (az * jax.nn.sigmoid(az)).astype(bf16)
        for h in range(N_HEADS):
            vt_ref[0, r0 // sub, h * V_EXT:h * V_EXT + V_DIM, :] = vt[h * V_DIM:(h + 1) * V_DIM, :]
            vt_ref[0, r0 // sub, h * V_EXT + V_DIM:(h + 1) * V_EXT, :] = ones_rows

    ubuf_ref[...] = prev


def _projection(x, pos3, g, w32, cw, cb, invf):
    b, s, d = x.shape
    tm = PROJ_TM
    tok = lambda bi, si: (bi, si, 0)
    out_tok = jax.ShapeDtypeStruct((b, s, GROUP), jnp.bfloat16)
    vt_rows = N_HEADS * V_EXT
    return pl.pallas_call(
        functools.partial(_proj_kernel, tm=tm, sub=PROJ_SUB),
        grid=(b, s // tm),
        in_specs=[
            pl.BlockSpec((1, tm, d), tok),
            pl.BlockSpec((1, 1, tm), lambda bi, si: (bi, 0, si)),
            _const_spec((1, d)),
            _const_spec(w32.shape),
            _const_spec(cw.shape),
            _const_spec(cb.shape),
            _const_spec(invf.shape),
        ],
        out_specs=[
            pl.BlockSpec((1, tm, GROUP), tok),
            pl.BlockSpec((1, tm, GROUP), tok),
            pl.BlockSpec((1, tm, GROUP), tok),
            pl.BlockSpec((1, tm // PROJ_SUB, vt_rows, PROJ_SUB), lambda bi, si: (bi, si, 0, 0)),
            pl.BlockSpec((1, tm, GROUP), tok),
        ],
        out_shape=[out_tok, out_tok, out_tok,
                   jax.ShapeDtypeStruct((b, s // PROJ_SUB, vt_rows, PROJ_SUB), jnp.bfloat16), out_tok],
        scratch_shapes=[
            pltpu.VMEM(w32.shape, jnp.bfloat16),
            pltpu.VMEM((GROUP, d), jnp.bfloat16),
            pltpu.VMEM((SUBLANES, GROUP), jnp.float32),
        ],
        compiler_params=pltpu.CompilerParams(
            dimension_semantics=("arbitrary", "arbitrary"), vmem_limit_bytes=VMEM_LIMIT),
        name="proj_conv_rope",
    )(x, pos3, g, w32, cw, cb, invf)


def _attn_kernel(q_ref, qn_ref, k_ref, kn_ref, vt_ref, gate_ref, lam_ref, g_ref, y_ref,
                 s_ref, mt_ref, qz_ref, acc_ref, m_ref, *, tq, tk, lam_init):
    qt = pl.program_id(1)
    first_step = jnp.logical_and(pl.program_id(0) == 0, qt == 0)
    n_chain = 2 * N_HEADS
    heads = [slice(h * LANES, (h + 1) * LANES) for h in range(N_HEADS)]
    lane = lax.broadcasted_iota(jnp.int32, (1, LANES), 1)
    every = slice(0, tq)
    upper = slice(tq // 2, tq)

    def tail_mask(key_tile, cols=every):
        n = cols.stop - cols.start
        kchunk = lax.broadcasted_iota(jnp.int32, (tk, n), 0) // CHUNK + key_tile * (tk // CHUNK)
        qchunk = (lax.broadcasted_iota(jnp.int32, (tk, n), 1) + cols.start) // CHUNK
        return kchunk <= qchunk

    def split_queries(src_ref):
        for h, hs in enumerate(heads):
            qh = src_ref[0, :, hs]
            zero = jnp.zeros_like(qh)
            qz_ref[2 * h] = jnp.where(lane < HEAD_DIM, qh, zero)
            qz_ref[2 * h + 1] = jnp.where(lane >= HEAD_DIM, qh, zero)

    acc_ref[...] = jnp.zeros_like(acc_ref)
    m_ref[...] = jnp.full_like(m_ref, NEG)

    def qk(j, c, slot, mask, cols=every, keys_ref=None):
        if keys_ref is None:
            kh = k_ref[0, pl.ds(pl.multiple_of(j * tk, tk), tk), heads[c // 2]]
        else:
            kh = keys_ref[0, :, heads[c // 2]]
        st = lax.dot_general(kh, qz_ref[c, cols, :], _NT, preferred_element_type=jnp.float32)
        if mask is not None:
            st = jnp.where(mask, st, NEG)
        s_ref[slot, c, :, cols] = st
        mt_ref[slot, c, :, cols] = jnp.max(st, axis=0, keepdims=True)

    def softmax_pv(j, c, slot, cols=every):
        m_old = m_ref[c, :, cols]
        m_new = jnp.maximum(m_old, mt_ref[slot, c, :, cols])
        alpha = jnp.exp2(m_old - m_new)
        p = jnp.exp2(s_ref[slot, c, :, cols] - m_new)
        m_ref[c, :, cols] = m_new
        h = c // 2
        vh = vt_ref[0, j, h * V_EXT:(h + 1) * V_EXT, :]
        acc_ref[c, :, cols] = alpha * acc_ref[c, :, cols] + jnp.dot(
            vh, p.astype(jnp.bfloat16), preferred_element_type=jnp.float32)

    lp = lam_ref[...]
    lam = (jnp.exp(jnp.sum(lp[0:1] * lp[1:2], axis=1, keepdims=True))
           - jnp.exp(jnp.sum(lp[2:3] * lp[3:4], axis=1, keepdims=True)) + lam_init)

    def finalize(h):
        hs = heads[h]
        a1, a2 = acc_ref[2 * h], acc_ref[2 * h + 1]
        o = (a1[:V_DIM] * (1.0 / a1[V_DIM:V_DIM + 1])
             - a2[:V_DIM] * (lam / a2[V_DIM:V_DIM + 1]))
        on = o * lax.rsqrt(jnp.mean(o * o, axis=0, keepdims=True) + SUBLN_EPS)
        y = on.T * (g_ref[...] * (1.0 - lam_init)) * gate_ref[0, :, hs].astype(jnp.float32)
        y_ref[0, :, hs] = y.astype(jnp.bfloat16)

    def stage(j, slot, has_next=True, next_mask=None, cols=every, next_cols=every, last=False):
        if last:
            split_queries(qn_ref)
        for c in range(n_chain):
            if has_next:
                qk(j + 1, c, 1 - slot, next_mask, next_cols)
            if last:
                qk(0, c, 1 - slot, None, keys_ref=kn_ref)
            softmax_pv(j, c, slot, cols)
            if last and c % 2 == 1:
                finalize(c // 2)

    @pl.when(first_step)
    def _():
        split_queries(q_ref)
        for c in range(n_chain):
            qk(0, c, 0, None)

    @pl.when(qt == 0)
    def _():
        mask = tail_mask(0)
        for c in range(n_chain):
            st = jnp.where(mask, s_ref[0, c], NEG)
            s_ref[0, c] = st
            mt_ref[0, c] = jnp.max(st, axis=0, keepdims=True)

    n_pairs = jnp.maximum(qt - 1, 0)

    def quad(i, carry):
        for t in range(4):
            stage(4 * i + t, t % 2)
        return carry

    lax.fori_loop(0, n_pairs // 2, quad, 0)

    @pl.when(n_pairs % 2 == 1)
    def _():
        stage(2 * n_pairs - 2, 0)
        stage(2 * n_pairs - 1, 1)

    def diagonal_stages():
        stage(2 * qt, 0, next_mask=tail_mask(1, upper), next_cols=upper)
        stage(2 * qt + 1, 1, has_next=False, cols=upper, last=True)

    @pl.when(qt > 0)
    def _():
        stage(2 * qt - 2, 0)
        stage(2 * qt - 1, 1, next_mask=tail_mask(0))
        diagonal_stages()

    @pl.when(qt == 0)
    def _():
        diagonal_stages()


def _attention(q, k, vt, gate, lam_params, subln_g, lam_init):
    b, s, w = q.shape
    tq, tk = ATTN_TQ, ATTN_TK
    assert tq == 2 * tk and tk % CHUNK == 0 and vt.shape[3] == tk
    n_chain = 2 * N_HEADS
    f32 = jnp.float32
    nq = s // tq
    qtile = lambda bi, qi: (bi, qi, 0)
    next_b = lambda bi, qi: jnp.minimum(bi + (qi + 1) // nq, b - 1)
    return pl.pallas_call(
        functools.partial(_attn_kernel, tq=tq, tk=tk, lam_init=lam_init),
        grid=(b, nq),
        in_specs=[
            pl.BlockSpec((1, tq, w), qtile),
            pl.BlockSpec((1, tq, w), lambda bi, qi: (next_b(bi, qi), (qi + 1) % nq, 0)),
            pl.BlockSpec((1, s, w), lambda bi, qi: (bi, 0, 0)),
            pl.BlockSpec((1, tk, w), lambda bi, qi: (next_b(bi, qi), 0, 0)),
            pl.BlockSpec((1,) + vt.shape[1:], lambda bi, qi: (bi, 0, 0, 0)),
            pl.BlockSpec((1, tq, w), qtile),
            _const_spec(lam_params.shape),
            _const_spec(subln_g.shape),
        ],
        out_specs=pl.BlockSpec((1, tq, w), qtile),
        out_shape=jax.ShapeDtypeStruct((b, s, w), jnp.bfloat16),
        scratch_shapes=[
            pltpu.VMEM((2, n_chain, tk, tq), f32),
            pltpu.VMEM((2, n_chain, 1, tq), f32),
            pltpu.VMEM((n_chain, tq, LANES), jnp.bfloat16),
            pltpu.VMEM((n_chain, V_EXT, tq), f32),
            pltpu.VMEM((n_chain, 1, tq), f32),
        ],
        compiler_params=pltpu.CompilerParams(
            dimension_semantics=("arbitrary", "arbitrary"), vmem_limit_bytes=VMEM_LIMIT),
        name="diff_attention",
    )(q, q, k, k, vt, gate, lam_params, subln_g)


def _rms(h, g):
    return h * lax.rsqrt(jnp.mean(h * h, axis=-1, keepdims=True) + EPS) * g


def _out_kernel(x_ref, yc_ref, ya_ref, p_ref, wo_ref, wg_ref, wp_ref, fn_ref, o_ref):
    f32, bf16 = jnp.float32, jnp.bfloat16
    tm = x_ref.shape[0]
    halves = [slice(r0, r0 + OUT_SUB) for r0 in range(0, tm, OUT_SUB)]
    mix = [jnp.dot(yc_ref[rows, :], wo_ref[0:CONV_WIDTH, :], preferred_element_type=f32)
           + jnp.dot(ya_ref[rows, :], wo_ref[CONV_WIDTH:, :], preferred_element_type=f32)
           for rows in halves]
    h, z, pp = [], [], []
    for rows, m in zip(halves, mix):
        h.append(x_ref[rows, :] + m)
        z.append(jnp.dot(h[-1].astype(bf16), wg_ref[...], preferred_element_type=f32))
        pp.append(jnp.dot(p_ref[rows, :].astype(bf16), wp_ref[...], preferred_element_type=f32))
    for rows, hh, zz, pr in zip(halves, h, z, pp):
        inv = lax.rsqrt(jnp.mean(hh * hh, axis=-1, keepdims=True) + EPS)
        o_ref[rows, :] = _rms(hh + jax.nn.sigmoid(zz * inv) * pr, fn_ref[...])


def _output(x2, yc2, ya2, p2, wo_bf, wg_bf, wp_bf, final_norm):
    t, d = x2.shape
    tm = OUT_TM
    row = lambda i: (i, 0)
    return pl.pallas_call(
        _out_kernel,
        grid=(t // tm,),
        in_specs=[
            pl.BlockSpec((tm, d), row),
            pl.BlockSpec((tm, CONV_WIDTH), row),
            pl.BlockSpec((tm, ATTN_WIDTH), row),
            pl.BlockSpec((tm, PLE_DIM), row),
            _const_spec(wo_bf.shape),
            _const_spec(wg_bf.shape),
            _const_spec(wp_bf.shape),
            _const_spec(final_norm.shape),
        ],
        out_specs=pl.BlockSpec((tm, d), row),
        out_shape=jax.ShapeDtypeStruct((t, d), jnp.float32),
        compiler_params=pltpu.CompilerParams(
            dimension_semantics=("parallel",), vmem_limit_bytes=VMEM_LIMIT),
        name="out_ple_norm",
    )(x2, yc2, ya2, p2, wo_bf, wg_bf, wp_bf, final_norm)


def kernel(x, p, positions, norm_mix, w_in, conv_w, conv_b, lambda_q1, lambda_k1, lambda_q2,
           lambda_k2, subln_g, w_out, norm_ple, w_ple_gate, w_ple_proj, final_norm):
    b, s, d = x.shape
    depth = p.shape[0]
    assert depth == 1 and d == D_MODEL and w_in.shape[-1] == 8 * GROUP
    assert s % PROJ_TM == 0 and s % ATTN_TQ == 0 and (b * s) % OUT_TM == 0
    bf16 = jnp.bfloat16
    lam_init = 0.8 - 0.6 * math.exp(-0.3 * 0)

    half = ROT_DIM // 2
    invf = (ROPE_THETA ** (-jnp.arange(half, dtype=jnp.float32) / half))[:, None]
    yconv, q, k, vt, gate = _projection(
        x, positions[:, None, :], norm_mix[0][None, :], w_in[0],
        conv_w[0], conv_b[0][None, :], invf)

    lam_params = jnp.stack([lambda_q1[0], lambda_k1[0], lambda_q2[0], lambda_k2[0]])
    yattn = _attention(q, k, vt, gate, lam_params, subln_g[0][None, :], lam_init)

    out = _output(
        x.reshape(b * s, d), yconv.reshape(b * s, CONV_WIDTH), yattn.reshape(b * s, ATTN_WIDTH),
        p[0].reshape(b * s, PLE_DIM), w_out[0].astype(bf16),
        (norm_ple[0][:, None] * w_ple_gate[0]).astype(bf16),
        w_ple_proj[0].astype(bf16), final_norm[None, :])
    return out.reshape(b, s, d)
```

```python
import functools
import math

import jax
import jax.numpy as jnp
from jax import lax
from jax.experimental import pallas as pl
from jax.experimental.pallas import tpu as pltpu

D_MODEL = 1024
CHUNK = 64
PLE_DIM = 256
CONV_WIDTH = 512
CONV_K = 3
ATTN_WIDTH = 512
N_HEADS = 4
HEAD_DIM = 64
V_DIM = 2 * HEAD_DIM
V_EXT = V_DIM + 16
ROT_DIM = HEAD_DIM // 4
ROPE_THETA = 500000.0
EPS = 1e-6
SUBLN_EPS = 1e-5
GROUP = 512
V_GROUP = 6
LANES = 128
SUBLANES = 8
NEG = -1e30

PROJ_TM = 1024
PROJ_SUB = 256
ATTN_TQ = 512
ATTN_TK = 256
OUT_TM = 1024
OUT_SUB = 256
VMEM_LIMIT = 56 * 1024 * 1024

_NT = (((1,), (1,)), ((), ()))


def _const_spec(shape):
    return pl.BlockSpec(shape, lambda *_: (0,) * len(shape), pipeline_mode=pl.Buffered(1))


def _proj_kernel(x_ref, pos_ref, g_ref, w32_ref, cw_ref, cb_ref, invf_ref,
                 yconv_ref, q_ref, k_ref, vt_ref, gate_ref, w_ref, wvt_ref, ubuf_ref, *, tm, sub):
    si = pl.program_id(1)
    bf16 = jnp.bfloat16

    @pl.when(jnp.logical_and(pl.program_id(0) == 0, si == 0))
    def _():
        for c in range(w32_ref.shape[1] // GROUP):
            cols = slice(c * GROUP, (c + 1) * GROUP)
            w_ref[:, cols] = w32_ref[:, cols].astype(bf16)
        wvt_ref[...] = w32_ref[:, V_GROUP * GROUP:(V_GROUP + 1) * GROUP].T.astype(bf16)

    @pl.when(si == 0)
    def _():
        ubuf_ref[...] = jnp.zeros_like(ubuf_ref)

    prev = ubuf_ref[...]
    row = lax.broadcasted_iota(jnp.int32, (SUBLANES, 1), 0)
    lane = lax.broadcasted_iota(jnp.int32, (1, LANES), 1)
    low = (lane % HEAD_DIM) < (ROT_DIM // 2)
    fill = HEAD_DIM - ROT_DIM
    one, zero = jnp.ones((fill, sub), jnp.float32), jnp.zeros((fill, sub), jnp.float32)
    ones_rows = jnp.ones((V_EXT - V_DIM, sub), bf16)

    for r0 in range(0, tm, sub):
        rows = slice(r0, r0 + sub)
        x = x_ref[0, rows, :]
        ms = jnp.mean(x * x, axis=-1, keepdims=True)
        u = (x * lax.rsqrt(ms + EPS) * g_ref[...]).astype(bf16)

        def proj(c):
            return jnp.dot(u, w_ref[:, c * GROUP:(c + 1) * GROUP], preferred_element_type=jnp.float32)

        uc = proj(2) * proj(0)

        def shifted(k):
            r = pltpu.roll(uc, k, 0)
            head = jnp.where(row < k, pltpu.roll(prev, k, 0), r[0:SUBLANES, :])
            return jnp.concatenate([head, r[SUBLANES:, :]], axis=0)

        conv = (cw_ref[0:1, :] * shifted(2) + cw_ref[1:2, :] * shifted(1)
                + cw_ref[2:3, :] * uc + cb_ref[...])
        prev = uc[sub - SUBLANES:, :]
        cz = proj(3)
        yconv_ref[0, rows, :] = (proj(1) * conv * (cz * jax.nn.sigmoid(cz))).astype(bf16)

        ang = invf_ref[...] * pos_ref[0, :, rows].astype(jnp.float32)
        c8, s8 = jnp.cos(ang), jnp.sin(ang)
        cos = jnp.concatenate([c8, c8, one] * 2, axis=0).T
        sin = jnp.concatenate([-s8, s8, zero] * 2, axis=0).T

        def rope(t, scale):
            outs = []
            for h in range(N_HEADS):
                th = t[:, h * LANES:(h + 1) * LANES]
                partner = jnp.where(low, pltpu.roll(th, LANES - ROT_DIM // 2, 1),
                                    pltpu.roll(th, ROT_DIM // 2, 1))
                r = th * cos + partner * sin
                outs.append(r * scale if scale != 1.0 else r)
            return jnp.concatenate(outs, axis=1)

        q_ref[0, rows, :] = rope(proj(4), HEAD_DIM ** -0.5 * math.log2(math.e)).astype(bf16)
        k_ref[0, rows, :] = rope(proj(5), 1.0).astype(bf16)
        az = proj(7)
        gate_ref[0, rows, :] = (az * jax.nn.sigmoid(az)).astype(bf16)
        vt = lax.dot_general(wvt_ref[...], u, _NT, preferred_element_type=jnp.float32).astype(bf16)
        for h in range(N_HEADS):
            vt_ref[0, r0 // sub, h * V_EXT:h * V_EXT + V_DIM, :] = vt[h * V_DIM:(h + 1) * V_DIM, :]
            vt_ref[0, r0 // sub, h * V_EXT + V_DIM:(h + 1) * V_EXT, :] = ones_rows

    ubuf_ref[...] = prev


def _projection(x, pos3, g, w32, cw, cb, invf):
    b, s, d = x.shape
    tm = PROJ_TM
    tok = lambda bi, si: (bi, si, 0)
    out_tok = jax.ShapeDtypeStruct((b, s, GROUP), jnp.bfloat16)
    vt_rows = N_HEADS * V_EXT
    return pl.pallas_call(
        functools.partial(_proj_kernel, tm=tm, sub=PROJ_SUB),
        grid=(b, s // tm),
        in_specs=[
            pl.BlockSpec((1, tm, d), tok),
            pl.BlockSpec((1, 1, tm), lambda bi, si: (bi, 0, si)),
            _const_spec((1, d)),
            _const_spec(w32.shape),
            _const_spec(cw.shape),
            _const_spec(cb.shape),
            _const_spec(invf.shape),
        ],
        out_specs=[
            pl.BlockSpec((1, tm, GROUP), tok),
            pl.BlockSpec((1, tm, GROUP), tok),
            pl.BlockSpec((1, tm, GROUP), tok),
            pl.BlockSpec((1, tm // PROJ_SUB, vt_rows, PROJ_SUB), lambda bi, si: (bi, si, 0, 0)),
            pl.BlockSpec((1, tm, GROUP), tok),
        ],
        out_shape=[out_tok, out_tok, out_tok,
                   jax.ShapeDtypeStruct((b, s // PROJ_SUB, vt_rows, PROJ_SUB), jnp.bfloat16), out_tok],
        scratch_shapes=[
            pltpu.VMEM(w32.shape, jnp.bfloat16),
            pltpu.VMEM((GROUP, d), jnp.bfloat16),
            pltpu.VMEM((SUBLANES, GROUP), jnp.float32),
        ],
        compiler_params=pltpu.CompilerParams(
            dimension_semantics=("arbitrary", "arbitrary"), vmem_limit_bytes=VMEM_LIMIT),
        name="proj_conv_rope",
    )(x, pos3, g, w32, cw, cb, invf)


def _attn_kernel(q_ref, qn_ref, k_ref, kn_ref, vt_ref, gate_ref, lam_ref, g_ref, y_ref,
                 s_ref, mt_ref, qz_ref, acc_ref, m_ref, *, tq, tk, lam_init):
    qt = pl.program_id(1)
    first_step = jnp.logical_and(pl.program_id(0) == 0, qt == 0)
    n_chain = 2 * N_HEADS
    heads = [slice(h * LANES, (h + 1) * LANES) for h in range(N_HEADS)]
    lane = lax.broadcasted_iota(jnp.int32, (1, LANES), 1)
    every = slice(0, tq)
    upper = slice(tq // 2, tq)

    def tail_mask(key_tile, cols=every):
        n = cols.stop - cols.start
        kchunk = lax.broadcasted_iota(jnp.int32, (tk, n), 0) // CHUNK + key_tile * (tk // CHUNK)
        qchunk = (lax.broadcasted_iota(jnp.int32, (tk, n), 1) + cols.start) // CHUNK
        return kchunk <= qchunk

    def split_queries(src_ref):
        for h, hs in enumerate(heads):
            qh = src_ref[0, :, hs]
            zero = jnp.zeros_like(qh)
            qz_ref[2 * h] = jnp.where(lane < HEAD_DIM, qh, zero)
            qz_ref[2 * h + 1] = jnp.where(lane >= HEAD_DIM, qh, zero)

    acc_ref[...] = jnp.zeros_like(acc_ref)
    m_ref[...] = jnp.full_like(m_ref, NEG)

    def qk(j, c, slot, mask, cols=every, keys_ref=None):
        if keys_ref is None:
            kh = k_ref[0, pl.ds(pl.multiple_of(j * tk, tk), tk), heads[c // 2]]
        else:
            kh = keys_ref[0, :, heads[c // 2]]
        st = lax.dot_general(kh, qz_ref[c, cols, :], _NT, preferred_element_type=jnp.float32)
        if mask is not None:
            st = jnp.where(mask, st, NEG)
        s_ref[slot, c, :, cols] = st
        mt_ref[slot, c, :, cols] = jnp.max(st, axis=0, keepdims=True)

    def softmax_pv(j, c, slot, cols=every):
        m_old = m_ref[c, :, cols]
        m_new = jnp.maximum(m_old, mt_ref[slot, c, :, cols])
        alpha = jnp.exp2(m_old - m_new)
        p = jnp.exp2(s_ref[slot, c, :, cols] - m_new)
        m_ref[c, :, cols] = m_new
        h = c // 2
        vh = vt_ref[0, j, h * V_EXT:(h + 1) * V_EXT, :]
        acc_ref[c, :, cols] = alpha * acc_ref[c, :, cols] + jnp.dot(
            vh, p.astype(jnp.bfloat16), preferred_element_type=jnp.float32)

    lp = lam_ref[...]
    lam = (jnp.exp(jnp.sum(lp[0:1] * lp[1:2], axis=1, keepdims=True))
           - jnp.exp(jnp.sum(lp[2:3] * lp[3:4], axis=1, keepdims=True)) + lam_init)

    def finalize(h):
        hs = heads[h]
        a1, a2 = acc_ref[2 * h], acc_ref[2 * h + 1]
        o = (a1[:V_DIM] * (1.0 / a1[V_DIM:V_DIM + 1])
             - a2[:V_DIM] * (lam / a2[V_DIM:V_DIM + 1]))
        on = o * lax.rsqrt(jnp.mean(o * o, axis=0, keepdims=True) + SUBLN_EPS)
        y = on.T * (g_ref[...] * (1.0 - lam_init)) * gate_ref[0, :, hs].astype(jnp.float32)
        y_ref[0, :, hs] = y.astype(jnp.bfloat16)

    def stage(j, slot, has_next=True, next_mask=None, cols=every, next_cols=every, last=False):
        if last:
            split_queries(qn_ref)
        for c in range(n_chain):
            if has_next:
                qk(j + 1, c, 1 - slot, next_mask, next_cols)
            if last:
                qk(0, c, 1 - slot, None, keys_ref=kn_ref)
            softmax_pv(j, c, slot, cols)
            if last and c % 2 == 1:
                finalize(c // 2)

    @pl.when(first_step)
    def _():
        split_queries(q_ref)
        for c in range(n_chain):
            qk(0, c, 0, None)

    @pl.when(qt == 0)
    def _():
        mask = tail_mask(0)
        for c in range(n_chain):
            st = jnp.where(mask, s_ref[0, c], NEG)
            s_ref[0, c] = st
            mt_ref[0, c] = jnp.max(st, axis=0, keepdims=True)

    n_pairs = jnp.maximum(qt - 1, 0)

    def quad(i, carry):
        for t in range(4):
            stage(4 * i + t, t % 2)
        return carry

    lax.fori_loop(0, n_pairs // 2, quad, 0)

    @pl.when(n_pairs % 2 == 1)
    def _():
        stage(2 * n_pairs - 2, 0)
        stage(2 * n_pairs - 1, 1)

    def diagonal_stages():
        stage(2 * qt, 0, next_mask=tail_mask(1, upper), next_cols=upper)
        stage(2 * qt + 1, 1, has_next=False, cols=upper, last=True)

    @pl.when(qt > 0)
    def _():
        stage(2 * qt - 2, 0)
        stage(2 * qt - 1, 1, next_mask=tail_mask(0))
        diagonal_stages()

    @pl.when(qt == 0)
    def _():
        diagonal_stages()


def _attention(q, k, vt, gate, lam_params, subln_g, lam_init):
    b, s, w = q.shape
    tq, tk = ATTN_TQ, ATTN_TK
    assert tq == 2 * tk and tk % CHUNK == 0 and vt.shape[3] == tk
    n_chain = 2 * N_HEADS
    f32 = jnp.float32
    nq = s // tq
    qtile = lambda bi, qi: (bi, qi, 0)
    next_b = lambda bi, qi: jnp.minimum(bi + (qi + 1) // nq, b - 1)
    return pl.pallas_call(
        functools.partial(_attn_kernel, tq=tq, tk=tk, lam_init=lam_init),
        grid=(b, nq),
        in_specs=[
            pl.BlockSpec((1, tq, w), qtile),
            pl.BlockSpec((1, tq, w), lambda bi, qi: (next_b(bi, qi), (qi + 1) % nq, 0)),
            pl.BlockSpec((1, s, w), lambda bi, qi: (bi, 0, 0)),
            pl.BlockSpec((1, tk, w), lambda bi, qi: (next_b(bi, qi), 0, 0)),
            pl.BlockSpec((1,) + vt.shape[1:], lambda bi, qi: (bi, 0, 0, 0)),
            pl.BlockSpec((1, tq, w), qtile),
            _const_spec(lam_params.shape),
            _const_spec(subln_g.shape),
        ],
        out_specs=pl.BlockSpec((1, tq, w), qtile),
        out_shape=jax.ShapeDtypeStruct((b, s, w), jnp.bfloat16),
        scratch_shapes=[
            pltpu.VMEM((2, n_chain, tk, tq), f32),
            pltpu.VMEM((2, n_chain, 1, tq), f32),
            pltpu.VMEM((n_chain, tq, LANES), jnp.bfloat16),
            pltpu.VMEM((n_chain, V_EXT, tq), f32),
            pltpu.VMEM((n_chain, 1, tq), f32),
        ],
        compiler_params=pltpu.CompilerParams(
            dimension_semantics=("arbitrary", "arbitrary"), vmem_limit_bytes=VMEM_LIMIT),
        name="diff_attention",
    )(q, q, k, k, vt, gate, lam_params, subln_g)


def _rms(h, g):
    return h * lax.rsqrt(jnp.mean(h * h, axis=-1, keepdims=True) + EPS) * g


def _out_kernel(x_ref, yc_ref, ya_ref, p_ref, wo_ref, wg_ref, wp_ref, fn_ref, o_ref):
    f32, bf16 = jnp.float32, jnp.bfloat16
    tm = x_ref.shape[0]
    halves = [slice(r0, r0 + OUT_SUB) for r0 in range(0, tm, OUT_SUB)]
    mix = [jnp.dot(yc_ref[rows, :], wo_ref[0:CONV_WIDTH, :], preferred_element_type=f32)
           + jnp.dot(ya_ref[rows, :], wo_ref[CONV_WIDTH:, :], preferred_element_type=f32)
           for rows in halves]
    h, z, pp = [], [], []
    for rows, m in zip(halves, mix):
        h.append(x_ref[rows, :] + m)
        z.append(jnp.dot(h[-1].astype(bf16), wg_ref[...], preferred_element_type=f32))
        pp.append(jnp.dot(p_ref[rows, :].astype(bf16), wp_ref[...], preferred_element_type=f32))
    for rows, hh, zz, pr in zip(halves, h, z, pp):
        inv = lax.rsqrt(jnp.mean(hh * hh, axis=-1, keepdims=True) + EPS)
        o_ref[rows, :] = _rms(hh + jax.nn.sigmoid(zz * inv) * pr, fn_ref[...])


def _output(x2, yc2, ya2, p2, wo_bf, wg_bf, wp_bf, final_norm):
    t, d = x2.shape
    tm = OUT_TM
    row = lambda i: (i, 0)
    return pl.pallas_call(
        _out_kernel,
        grid=(t // tm,),
        in_specs=[
            pl.BlockSpec((tm, d), row),
            pl.BlockSpec((tm, CONV_WIDTH), row),
            pl.BlockSpec((tm, ATTN_WIDTH), row),
            pl.BlockSpec((tm, PLE_DIM), row),
            _const_spec(wo_bf.shape),
            _const_spec(wg_bf.shape),
            _const_spec(wp_bf.shape),
            _const_spec(final_norm.shape),
        ],
        out_specs=pl.BlockSpec((tm, d), row),
        out_shape=jax.ShapeDtypeStruct((t, d), jnp.float32),
        compiler_params=pltpu.CompilerParams(
            dimension_semantics=("parallel",), vmem_limit_bytes=VMEM_LIMIT),
        name="out_ple_norm",
    )(x2, yc2, ya2, p2, wo_bf, wg_bf, wp_bf, final_norm)


def kernel(x, p, positions, norm_mix, w_in, conv_w, conv_b, lambda_q1, lambda_k1, lambda_q2,
           lambda_k2, subln_g, w_out, norm_ple, w_ple_gate, w_ple_proj, final_norm):
    b, s, d = x.shape
    depth = p.shape[0]
    assert depth == 1 and d == D_MODEL and w_in.shape[-1] == 8 * GROUP
    assert s % PROJ_TM == 0 and s % ATTN_TQ == 0 and (b * s) % OUT_TM == 0
    bf16 = jnp.bfloat16
    lam_init = 0.8 - 0.6 * math.exp(-0.3 * 0)

    half = ROT_DIM // 2
    invf = (ROPE_THETA ** (-jnp.arange(half, dtype=jnp.float32) / half))[:, None]
    yconv, q, k, vt, gate = _projection(
        x, positions[:, None, :], norm_mix[0][None, :], w_in[0],
        conv_w[0], conv_b[0][None, :], invf)

    lam_params = jnp.stack([lambda_q1[0], lambda_k1[0], lambda_q2[0], lambda_k2[0]])
    yattn = _attention(q, k, vt, gate, lam_params, subln_g[0][None, :], lam_init)

    out = _output(
        x.reshape(b * s, d), yconv.reshape(b * s, CONV_WIDTH), yattn.reshape(b * s, ATTN_WIDTH),
        p[0].reshape(b * s, PLE_DIM), w_out[0].astype(bf16),
        (norm_ple[0][:, None] * w_ple_gate[0]).astype(bf16),
        w_ple_proj[0].astype(bf16), final_norm[None, :])
    return out.reshape(b, s, d)
```

```python
import functools
import math

import jax
import jax.numpy as jnp
from jax import lax
from jax.experimental import pallas as pl
from jax.experimental.pallas import tpu as pltpu

D_MODEL = 1024
CHUNK = 64
PLE_DIM = 256
CONV_WIDTH = 512
CONV_K = 3
ATTN_WIDTH = 512
N_HEADS = 4
HEAD_DIM = 64
V_DIM = 2 * HEAD_DIM
V_EXT = V_DIM + 16
ROT_DIM = HEAD_DIM // 4
ROPE_THETA = 500000.0
EPS = 1e-6
SUBLN_EPS = 1e-5
GROUP = 512
V_GROUP = 6
LANES = 128
SUBLANES = 8
NEG = -1e30

PROJ_TM = 1024
PROJ_SUB = 256
ATTN_TQ = 512
ATTN_TK = 256
OUT_TM = 1024
OUT_SUB = 256
VMEM_LIMIT = 56 * 1024 * 1024

_NT = (((1,), (1,)), ((), ()))


def _const_spec(shape):
    return pl.BlockSpec(shape, lambda *_: (0,) * len(shape), pipeline_mode=pl.Buffered(1))


def _proj_kernel(x_ref, pos_ref, g_ref, w32_ref, cw_ref, cb_ref, invf_ref,
                 yconv_ref, q_ref, k_ref, vt_ref, gate_ref, w_ref, wvt_ref, ubuf_ref, *, tm, sub):
    si = pl.program_id(1)
    bf16 = jnp.bfloat16

    @pl.when(jnp.logical_and(pl.program_id(0) == 0, si == 0))
    def _():
        for c in range(w32_ref.shape[1] // GROUP):
            cols = slice(c * GROUP, (c + 1) * GROUP)
            w_ref[:, cols] = w32_ref[:, cols].astype(bf16)
        wvt_ref[...] = w32_ref[:, V_GROUP * GROUP:(V_GROUP + 1) * GROUP].T.astype(bf16)

    @pl.when(si == 0)
    def _():
        ubuf_ref[...] = jnp.zeros_like(ubuf_ref)

    prev = ubuf_ref[...]
    row = lax.broadcasted_iota(jnp.int32, (SUBLANES, 1), 0)
    lane = lax.broadcasted_iota(jnp.int32, (1, LANES), 1)
    low = (lane % HEAD_DIM) < (ROT_DIM // 2)
    fill = HEAD_DIM - ROT_DIM
    one, zero = jnp.ones((fill, sub), jnp.float32), jnp.zeros((fill, sub), jnp.float32)
    ones_rows = jnp.ones((V_EXT - V_DIM, sub), bf16)

    for r0 in range(0, tm, sub):
        rows = slice(r0, r0 + sub)
        x = x_ref[0, rows, :]
        ms = jnp.mean(x * x, axis=-1, keepdims=True)
        u = (x * lax.rsqrt(ms + EPS) * g_ref[...]).astype(bf16)

        def proj(c):
            return jnp.dot(u, w_ref[:, c * GROUP:(c + 1) * GROUP], preferred_element_type=jnp.float32)

        uc = proj(2) * proj(0)

        def shifted(k):
            r = pltpu.roll(uc, k, 0)
            head = jnp.where(row < k, pltpu.roll(prev, k, 0), r[0:SUBLANES, :])
            return jnp.concatenate([head, r[SUBLANES:, :]], axis=0)

        conv = (cw_ref[0:1, :] * shifted(2) + cw_ref[1:2, :] * shifted(1)
                + cw_ref[2:3, :] * uc + cb_ref[...])
        prev = uc[sub - SUBLANES:, :]
        cz = proj(3)
        yconv_ref[0, rows, :] = (proj(1) * conv * (cz * jax.nn.sigmoid(cz))).astype(bf16)

        ang = invf_ref[...] * pos_ref[0, :, rows].astype(jnp.float32)
        c8, s8 = jnp.cos(ang), jnp.sin(ang)
        cos = jnp.concatenate([c8, c8, one] * 2, axis=0).T
        sin = jnp.concatenate([-s8, s8, zero] * 2, axis=0).T

        def rope(t, scale):
            outs = []
            for h in range(N_HEADS):
                th = t[:, h * LANES:(h + 1) * LANES]
                partner = jnp.where(low, pltpu.roll(th, LANES - ROT_DIM // 2, 1),
                                    pltpu.roll(th, ROT_DIM // 2, 1))
                r = th * cos + partner * sin
                outs.append(r * scale if scale != 1.0 else r)
            return jnp.concatenate(outs, axis=1)

        q_ref[0, rows, :] = rope(proj(4), HEAD_DIM ** -0.5 * math.log2(math.e)).astype(bf16)
        k_ref[0, rows, :] = rope(proj(5), 1.0).astype(bf16)
        az = proj(7)
        gate_ref[0, rows, :] = (az * jax.nn.sigmoid(az)).astype(bf16)
        vt = lax.dot_general(wvt_ref[...], u, _NT, preferred_element_type=jnp.float32).astype(bf16)
        for h in range(N_HEADS):
            vt_ref[0, r0 // sub, h * V_EXT:h * V_EXT + V_DIM, :] = vt[h * V_DIM:(h + 1) * V_DIM, :]
            vt_ref[0, r0 // sub, h * V_EXT + V_DIM:(h + 1) * V_EXT, :] = ones_rows

    ubuf_ref[...] = prev


def _projection(x, pos3, g, w32, cw, cb, invf):
    b, s, d = x.shape
    tm = PROJ_TM
    tok = lambda bi, si: (bi, si, 0)
    out_tok = jax.ShapeDtypeStruct((b, s, GROUP), jnp.bfloat16)
    vt_rows = N_HEADS * V_EXT
    return pl.pallas_call(
        functools.partial(_proj_kernel, tm=tm, sub=PROJ_SUB),
        grid=(b, s // tm),
        in_specs=[
            pl.BlockSpec((1, tm, d), tok),
            pl.BlockSpec((1, 1, tm), lambda bi, si: (bi, 0, si)),
            _const_spec((1, d)),
            _const_spec(w32.shape),
            _const_spec(cw.shape),
            _const_spec(cb.shape),
            _const_spec(invf.shape),
        ],
        out_specs=[
            pl.BlockSpec((1, tm, GROUP), tok),
            pl.BlockSpec((1, tm, GROUP), tok),
            pl.BlockSpec((1, tm, GROUP), tok),
            pl.BlockSpec((1, tm // PROJ_SUB, vt_rows, PROJ_SUB), lambda bi, si: (bi, si, 0, 0)),
            pl.BlockSpec((1, tm, GROUP), tok),
        ],
        out_shape=[out_tok, out_tok, out_tok,
                   jax.ShapeDtypeStruct((b, s // PROJ_SUB, vt_rows, PROJ_SUB), jnp.bfloat16), out_tok],
        scratch_shapes=[
            pltpu.VMEM(w32.shape, jnp.bfloat16),
            pltpu.VMEM((GROUP, d), jnp.bfloat16),
            pltpu.VMEM((SUBLANES, GROUP), jnp.float32),
        ],
        compiler_params=pltpu.CompilerParams(
            dimension_semantics=("arbitrary", "arbitrary"), vmem_limit_bytes=VMEM_LIMIT),
        name="proj_conv_rope",
    )(x, pos3, g, w32, cw, cb, invf)


def _attn_kernel(q_ref, qn_ref, k_ref, kn_ref, vt_ref, gate_ref, lam_ref, g_ref, y_ref,
                 s_ref, mt_ref, qz_ref, acc_ref, m_ref, sx_ref, mtx_ref, *, tq, tk, lam_init):
    qt = pl.program_id(1)
    first_step = jnp.logical_and(pl.program_id(0) == 0, qt == 0)
    n_chain = 2 * N_HEADS
    heads = [slice(h * LANES, (h + 1) * LANES) for h in range(N_HEADS)]
    lane = lax.broadcasted_iota(jnp.int32, (1, LANES), 1)
    every = slice(0, tq)
    lower = slice(0, tq // 2)
    upper = slice(tq // 2, tq)

    def tail_mask(key_tile, cols=every):
        n = cols.stop - cols.start
        kchunk = lax.broadcasted_iota(jnp.int32, (tk, n), 0) // CHUNK + key_tile * (tk // CHUNK)
        qchunk = (lax.broadcasted_iota(jnp.int32, (tk, n), 1) + cols.start) // CHUNK
        return kchunk <= qchunk

    def split_queries(src_ref):
        for h, hs in enumerate(heads):
            qh = src_ref[0, :, hs]
            zero = jnp.zeros_like(qh)
            qz_ref[2 * h] = jnp.where(lane < HEAD_DIM, qh, zero)
            qz_ref[2 * h + 1] = jnp.where(lane >= HEAD_DIM, qh, zero)

    acc_ref[...] = jnp.zeros_like(acc_ref)
    m_ref[...] = jnp.full_like(m_ref, NEG)

    def qk(j, c, slot, mask, cols=every, keys_ref=None):
        if keys_ref is None:
            kh = k_ref[0, pl.ds(pl.multiple_of(j * tk, tk), tk), heads[c // 2]]
        else:
            kh = keys_ref[0, :, heads[c // 2]]
        st = lax.dot_general(kh, qz_ref[c, cols, :], _NT, preferred_element_type=jnp.float32)
        if mask is not None:
            st = jnp.where(mask, st, NEG)
        s_ref[slot, c, :, cols] = st
        mt_ref[slot, c, :, cols] = jnp.max(st, axis=0, keepdims=True)

    def softmax_pv(j, c, slot, cols=every):
        m_old = m_ref[c, :, cols]
        m_new = jnp.maximum(m_old, mt_ref[slot, c, :, cols])
        alpha = jnp.exp2(m_old - m_new)
        p = jnp.exp2(s_ref[slot, c, :, cols] - m_new)
        m_ref[c, :, cols] = m_new
        h = c // 2
        vh = vt_ref[0, j, h * V_EXT:(h + 1) * V_EXT, :]
        acc_ref[c, :, cols] = alpha * acc_ref[c, :, cols] + jnp.dot(
            vh, p.astype(jnp.bfloat16), preferred_element_type=jnp.float32)

    lp = lam_ref[...]
    lam = (jnp.exp(jnp.sum(lp[0:1] * lp[1:2], axis=1, keepdims=True))
           - jnp.exp(jnp.sum(lp[2:3] * lp[3:4], axis=1, keepdims=True)) + lam_init)

    def finalize(h):
        hs = heads[h]
        a1, a2 = acc_ref[2 * h], acc_ref[2 * h + 1]
        o = (a1[:V_DIM] * (1.0 / a1[V_DIM:V_DIM + 1])
             - a2[:V_DIM] * (lam / a2[V_DIM:V_DIM + 1]))
        on = o * lax.rsqrt(jnp.mean(o * o, axis=0, keepdims=True) + SUBLN_EPS)
        y = on.T * (g_ref[...] * (1.0 - lam_init)) * gate_ref[0, :, hs].astype(jnp.float32)
        y_ref[0, :, hs] = y.astype(jnp.bfloat16)

    def qk_upper(j, c, mask):
        kh = k_ref[0, pl.ds(pl.multiple_of(j * tk, tk), tk), heads[c // 2]]
        st = lax.dot_general(kh, qz_ref[c, upper, :], _NT, preferred_element_type=jnp.float32)
        st = jnp.where(mask, st, NEG)
        sx_ref[c] = st
        mtx_ref[c] = jnp.max(st, axis=0, keepdims=True)

    def diagonal_pv(j, c):
        h = c // 2
        ve = slice(h * V_EXT, (h + 1) * V_EXT)
        bf16 = jnp.bfloat16
        m_old = m_ref[c, :, lower]
        m_new = jnp.maximum(m_old, mt_ref[0, c, :, lower])
        p = jnp.exp2(s_ref[0, c, :, lower] - m_new)
        acc_ref[c, :, lower] = jnp.exp2(m_old - m_new) * acc_ref[c, :, lower] + jnp.dot(
            vt_ref[0, j, ve, :], p.astype(bf16), preferred_element_type=jnp.float32)
        m_old = m_ref[c, :, upper]
        m_new = jnp.maximum(jnp.maximum(m_old, mt_ref[0, c, :, upper]), mtx_ref[c])
        p0 = jnp.exp2(s_ref[0, c, :, upper] - m_new).astype(bf16)
        p1 = jnp.exp2(sx_ref[c] - m_new).astype(bf16)
        acc_ref[c, :, upper] = (jnp.exp2(m_old - m_new) * acc_ref[c, :, upper]
                                + jnp.dot(vt_ref[0, j, ve, :], p0, preferred_element_type=jnp.float32)
                                + jnp.dot(vt_ref[0, j + 1, ve, :], p1, preferred_element_type=jnp.float32))

    def stage(j, slot, has_next=True, next_mask=None, cols=every, next_cols=every, last=False):
        if last:
            split_queries(qn_ref)
        for c in range(n_chain):
            if has_next:
                qk(j + 1, c, 1 - slot, next_mask, next_cols)
            softmax_pv(j, c, slot, cols)
            if last and c % 2 == 1:
                finalize(c // 2)
            if last:
                qk(0, c, 1 - slot, None, keys_ref=kn_ref)

    @pl.when(first_step)
    def _():
        split_queries(q_ref)
        for c in range(n_chain):
            qk(0, c, 0, None)

    @pl.when(qt == 0)
    def _():
        mask = tail_mask(0)
        for c in range(n_chain):
            st = jnp.where(mask, s_ref[0, c], NEG)
            s_ref[0, c] = st
            mt_ref[0, c] = jnp.max(st, axis=0, keepdims=True)

    n_pairs = jnp.maximum(qt - 1, 0)

    def quad(i, carry):
        for t in range(4):
            stage(4 * i + t, t % 2)
        return carry

    lax.fori_loop(0, n_pairs // 2, quad, 0)

    @pl.when(n_pairs % 2 == 1)
    def _():
        stage(2 * n_pairs - 2, 0)
        stage(2 * n_pairs - 1, 1)

    def diagonal_stages():
        stage(2 * qt, 0, next_mask=tail_mask(1, upper), next_cols=upper)
        stage(2 * qt + 1, 1, has_next=False, cols=upper, last=True)

    @pl.when(qt > 0)
    def _():
        stage(2 * qt - 2, 0)
        mask0, mask1 = tail_mask(0), tail_mask(1, upper)
        for c in range(n_chain):
            qk(2 * qt, c, 0, mask0)
            qk_upper(2 * qt + 1, c, mask1)
            softmax_pv(2 * qt - 1, c, 1)
        split_queries(qn_ref)
        for c in range(n_chain):
            diagonal_pv(2 * qt, c)
            if c % 2 == 1:
                finalize(c // 2)
            qk(0, c, 0, None, keys_ref=kn_ref)

    @pl.when(qt == 0)
    def _():
        diagonal_stages()


def _attention(q, k, vt, gate, lam_params, subln_g, lam_init):
    b, s, w = q.shape
    tq, tk = ATTN_TQ, ATTN_TK
    assert tq == 2 * tk and tk % CHUNK == 0 and vt.shape[3] == tk
    n_chain = 2 * N_HEADS
    f32 = jnp.float32
    nq = s // tq
    qtile = lambda bi, qi: (bi, qi, 0)
    next_b = lambda bi, qi: jnp.minimum(bi + (qi + 1) // nq, b - 1)
    return pl.pallas_call(
        functools.partial(_attn_kernel, tq=tq, tk=tk, lam_init=lam_init),
        grid=(b, nq),
        in_specs=[
            pl.BlockSpec((1, tq, w), qtile),
            pl.BlockSpec((1, tq, w), lambda bi, qi: (next_b(bi, qi), (qi + 1) % nq, 0)),
            pl.BlockSpec((1, s, w), lambda bi, qi: (bi, 0, 0)),
            pl.BlockSpec((1, tk, w), lambda bi, qi: (next_b(bi, qi), 0, 0)),
            pl.BlockSpec((1,) + vt.shape[1:], lambda bi, qi: (bi, 0, 0, 0)),
            pl.BlockSpec((1, tq, w), qtile),
            _const_spec(lam_params.shape),
            _const_spec(subln_g.shape),
        ],
        out_specs=pl.BlockSpec((1, tq, w), qtile),
        out_shape=jax.ShapeDtypeStruct((b, s, w), jnp.bfloat16),
        scratch_shapes=[
            pltpu.VMEM((2, n_chain, tk, tq), f32),
            pltpu.VMEM((2, n_chain, 1, tq), f32),
            pltpu.VMEM((n_chain, tq, LANES), jnp.bfloat16),
            pltpu.VMEM((n_chain, V_EXT, tq), f32),
            pltpu.VMEM((n_chain, 1, tq), f32),
            pltpu.VMEM((n_chain, tk, tq // 2), f32),
            pltpu.VMEM((n_chain, 1, tq // 2), f32),
        ],
        compiler_params=pltpu.CompilerParams(
            dimension_semantics=("arbitrary", "arbitrary"), vmem_limit_bytes=VMEM_LIMIT),
        name="diff_attention",
    )(q, q, k, k, vt, gate, lam_params, subln_g)


def _rms(h, g):
    return h * lax.rsqrt(jnp.mean(h * h, axis=-1, keepdims=True) + EPS) * g


def _out_kernel(x_ref, yc_ref, ya_ref, p_ref, wo_ref, wg_ref, wp_ref, fn_ref, o_ref):
    f32, bf16 = jnp.float32, jnp.bfloat16
    tm = x_ref.shape[0]
    halves = [slice(r0, r0 + OUT_SUB) for r0 in range(0, tm, OUT_SUB)]
    mix = [jnp.dot(yc_ref[rows, :], wo_ref[0:CONV_WIDTH, :], preferred_element_type=f32)
           + jnp.dot(ya_ref[rows, :], wo_ref[CONV_WIDTH:, :], preferred_element_type=f32)
           for rows in halves]
    h, z, pp = [], [], []
    for rows, m in zip(halves, mix):
        h.append(x_ref[rows, :] + m)
        z.append(jnp.dot(h[-1].astype(bf16), wg_ref[...], preferred_element_type=f32))
        pp.append(jnp.dot(p_ref[rows, :].astype(bf16), wp_ref[...], preferred_element_type=f32))
    for rows, hh, zz, pr in zip(halves, h, z, pp):
        inv = lax.rsqrt(jnp.mean(hh * hh, axis=-1, keepdims=True) + EPS)
        o_ref[rows, :] = _rms(hh + jax.nn.sigmoid(zz * inv) * pr, fn_ref[...])


def _output(x2, yc2, ya2, p2, wo_bf, wg_bf, wp_bf, final_norm):
    t, d = x2.shape
    tm = OUT_TM
    row = lambda i: (i, 0)
    return pl.pallas_call(
        _out_kernel,
        grid=(t // tm,),
        in_specs=[
            pl.BlockSpec((tm, d), row),
            pl.BlockSpec((tm, CONV_WIDTH), row),
            pl.BlockSpec((tm, ATTN_WIDTH), row),
            pl.BlockSpec((tm, PLE_DIM), row),
            _const_spec(wo_bf.shape),
            _const_spec(wg_bf.shape),
            _const_spec(wp_bf.shape),
            _const_spec(final_norm.shape),
        ],
        out_specs=pl.BlockSpec((tm, d), row),
        out_shape=jax.ShapeDtypeStruct((t, d), jnp.float32),
        compiler_params=pltpu.CompilerParams(
            dimension_semantics=("parallel",), vmem_limit_bytes=VMEM_LIMIT),
        name="out_ple_norm",
    )(x2, yc2, ya2, p2, wo_bf, wg_bf, wp_bf, final_norm)


def kernel(x, p, positions, norm_mix, w_in, conv_w, conv_b, lambda_q1, lambda_k1, lambda_q2,
           lambda_k2, subln_g, w_out, norm_ple, w_ple_gate, w_ple_proj, final_norm):
    b, s, d = x.shape
    depth = p.shape[0]
    assert depth == 1 and d == D_MODEL and w_in.shape[-1] == 8 * GROUP
    assert s % PROJ_TM == 0 and s % ATTN_TQ == 0 and (b * s) % OUT_TM == 0
    bf16 = jnp.bfloat16
    lam_init = 0.8 - 0.6 * math.exp(-0.3 * 0)

    half = ROT_DIM // 2
    invf = (ROPE_THETA ** (-jnp.arange(half, dtype=jnp.float32) / half))[:, None]
    yconv, q, k, vt, gate = _projection(
        x, positions[:, None, :], norm_mix[0][None, :], w_in[0],
        conv_w[0], conv_b[0][None, :], invf)

    lam_params = jnp.stack([lambda_q1[0], lambda_k1[0], lambda_q2[0], lambda_k2[0]])
    yattn = _attention(q, k, vt, gate, lam_params, subln_g[0][None, :], lam_init)

    out = _output(
        x.reshape(b * s, d), yconv.reshape(b * s, CONV_WIDTH), yattn.reshape(b * s, ATTN_WIDTH),
        p[0].reshape(b * s, PLE_DIM), w_out[0].astype(bf16),
        (norm_ple[0][:, None] * w_ple_gate[0]).astype(bf16),
        w_ple_proj[0].astype(bf16), final_norm[None, :])
    return out.reshape(b, s, d)
```

```python
import functools
import math

import jax
import jax.numpy as jnp
from jax import lax
from jax.experimental import pallas as pl
from jax.experimental.pallas import tpu as pltpu

D_MODEL = 1024
CHUNK = 64
PLE_DIM = 256
CONV_WIDTH = 512
CONV_K = 3
ATTN_WIDTH = 512
N_HEADS = 4
HEAD_DIM = 64
V_DIM = 2 * HEAD_DIM
V_EXT = V_DIM + 16
ROT_DIM = HEAD_DIM // 4
ROPE_THETA = 500000.0
EPS = 1e-6
SUBLN_EPS = 1e-5
GROUP = 512
V_GROUP = 6
LANES = 128
SUBLANES = 8
NEG = -1e30

PROJ_TM = 1024
PROJ_SUB = 256
ATTN_TQ = 512
ATTN_TK = 256
OUT_TM = 1024
OUT_SUB = 256
VMEM_LIMIT = 56 * 1024 * 1024

_NT = (((1,), (1,)), ((), ()))


def _const_spec(shape):
    return pl.BlockSpec(shape, lambda *_: (0,) * len(shape), pipeline_mode=pl.Buffered(1))


def _proj_kernel(x_ref, pos_ref, g_ref, w32_hbm, cw_ref, cb_ref, invf_ref,
                 yconv_ref, q_ref, k_ref, vt_ref, gate_ref, w_ref, wvt_ref, ubuf_ref,
                 stage_ref, sem, *, tm, sub):
    si = pl.program_id(1)
    bf16 = jnp.bfloat16
    n_groups = w32_hbm.shape[1] // GROUP

    def group_copy(c):
        return pltpu.make_async_copy(w32_hbm.at[:, pl.ds(c * GROUP, GROUP)],
                                     stage_ref.at[c % 2], sem.at[c % 2])

    @pl.when(jnp.logical_and(pl.program_id(0) == 0, si == 0))
    def _():
        group_copy(0).start()
        group_copy(1).start()
        for c in range(n_groups):
            group_copy(c).wait()
            w = stage_ref[c % 2]
            w_ref[:, c * GROUP:(c + 1) * GROUP] = w.astype(bf16)
            if c == V_GROUP:
                wvt_ref[...] = w.T.astype(bf16)
            if c + 2 < n_groups:
                group_copy(c + 2).start()

    @pl.when(si == 0)
    def _():
        ubuf_ref[...] = jnp.zeros_like(ubuf_ref)

    prev = ubuf_ref[...]
    row = lax.broadcasted_iota(jnp.int32, (SUBLANES, 1), 0)
    lane = lax.broadcasted_iota(jnp.int32, (1, LANES), 1)
    low = (lane % HEAD_DIM) < (ROT_DIM // 2)
    fill = HEAD_DIM - ROT_DIM
    one, zero = jnp.ones((fill, sub), jnp.float32), jnp.zeros((fill, sub), jnp.float32)
    ones_rows = jnp.ones((V_EXT - V_DIM, sub), bf16)

    for r0 in range(0, tm, sub):
        rows = slice(r0, r0 + sub)
        x = x_ref[0, rows, :]
        ms = jnp.mean(x * x, axis=-1, keepdims=True)
        u = (x * lax.rsqrt(ms + EPS) * g_ref[...]).astype(bf16)

        def proj(c):
            return jnp.dot(u, w_ref[:, c * GROUP:(c + 1) * GROUP], preferred_element_type=jnp.float32)

        uc = proj(2) * proj(0)

        def shifted(k):
            r = pltpu.roll(uc, k, 0)
            head = jnp.where(row < k, pltpu.roll(prev, k, 0), r[0:SUBLANES, :])
            return jnp.concatenate([head, r[SUBLANES:, :]], axis=0)

        conv = (cw_ref[0:1, :] * shifted(2) + cw_ref[1:2, :] * shifted(1)
                + cw_ref[2:3, :] * uc + cb_ref[...])
        prev = uc[sub - SUBLANES:, :]
        cz = proj(3)
        yconv_ref[0, rows, :] = (proj(1) * conv * (cz * jax.nn.sigmoid(cz))).astype(bf16)

        ang = invf_ref[...] * pos_ref[0, :, rows].astype(jnp.float32)
        c8, s8 = jnp.cos(ang), jnp.sin(ang)
        cos = jnp.concatenate([c8, c8, one] * 2, axis=0).T
        sin = jnp.concatenate([-s8, s8, zero] * 2, axis=0).T

        def rope(t, scale):
            outs = []
            for h in range(N_HEADS):
                th = t[:, h * LANES:(h + 1) * LANES]
                partner = jnp.where(low, pltpu.roll(th, LANES - ROT_DIM // 2, 1),
                                    pltpu.roll(th, ROT_DIM // 2, 1))
                r = th * cos + partner * sin
                outs.append(r * scale if scale != 1.0 else r)
            return jnp.concatenate(outs, axis=1)

        q_ref[0, rows, :] = rope(proj(4), HEAD_DIM ** -0.5 * math.log2(math.e)).astype(bf16)
        k_ref[0, rows, :] = rope(proj(5), 1.0).astype(bf16)
        az = proj(7)
        gate_ref[0, rows, :] = (az * jax.nn.sigmoid(az)).astype(bf16)
        vt = lax.dot_general(wvt_ref[...], u, _NT, preferred_element_type=jnp.float32).astype(bf16)
        for h in range(N_HEADS):
            vt_ref[0, h * V_EXT:h * V_EXT + V_DIM, rows] = vt[h * V_DIM:(h + 1) * V_DIM, :]
            vt_ref[0, h * V_EXT + V_DIM:(h + 1) * V_EXT, rows] = ones_rows

    ubuf_ref[...] = prev


def _projection(x, pos3, g, w32, cw, cb, invf):
    b, s, d = x.shape
    tm = PROJ_TM
    tok = lambda bi, si: (bi, si, 0)
    out_tok = jax.ShapeDtypeStruct((b, s, GROUP), jnp.bfloat16)
    vt_rows = N_HEADS * V_EXT
    return pl.pallas_call(
        functools.partial(_proj_kernel, tm=tm, sub=PROJ_SUB),
        grid=(b, s // tm),
        in_specs=[
            pl.BlockSpec((1, tm, d), tok),
            pl.BlockSpec((1, 1, tm), lambda bi, si: (bi, 0, si)),
            _const_spec((1, d)),
            pl.BlockSpec(memory_space=pl.ANY),
            _const_spec(cw.shape),
            _const_spec(cb.shape),
            _const_spec(invf.shape),
        ],
        out_specs=[
            pl.BlockSpec((1, tm, GROUP), tok),
            pl.BlockSpec((1, tm, GROUP), tok),
            pl.BlockSpec((1, tm, GROUP), tok),
            pl.BlockSpec((1, vt_rows, tm), lambda bi, si: (bi, 0, si)),
            pl.BlockSpec((1, tm, GROUP), tok),
        ],
        out_shape=[out_tok, out_tok, out_tok,
                   jax.ShapeDtypeStruct((b, vt_rows, s), jnp.bfloat16), out_tok],
        scratch_shapes=[
            pltpu.VMEM(w32.shape, jnp.bfloat16),
            pltpu.VMEM((GROUP, d), jnp.bfloat16),
            pltpu.VMEM((SUBLANES, GROUP), jnp.float32),
            pltpu.VMEM((2, d, GROUP), jnp.float32),
            pltpu.SemaphoreType.DMA((2,)),
        ],
        compiler_params=pltpu.CompilerParams(
            dimension_semantics=("arbitrary", "arbitrary"), vmem_limit_bytes=VMEM_LIMIT),
        name="proj_conv_rope",
    )(x, pos3, g, w32, cw, cb, invf)


def _attn_kernel(q_ref, qn_ref, k_ref, kn_ref, vt_ref, gate_ref, lam_ref, g_ref, y_ref,
                 s_ref, mt_ref, qz_ref, acc_ref, m_ref, *, tq, tk, lam_init):
    qt = pl.program_id(1)
    first_step = jnp.logical_and(pl.program_id(0) == 0, qt == 0)
    n_chain = 2 * N_HEADS
    heads = [slice(h * LANES, (h + 1) * LANES) for h in range(N_HEADS)]
    lane = lax.broadcasted_iota(jnp.int32, (1, LANES), 1)
    every = slice(0, tq)
    upper = slice(tq // 2, tq)

    def tail_mask(key_tile, cols=every):
        n = cols.stop - cols.start
        kchunk = lax.broadcasted_iota(jnp.int32, (tk, n), 0) // CHUNK + key_tile * (tk // CHUNK)
        qchunk = (lax.broadcasted_iota(jnp.int32, (tk, n), 1) + cols.start) // CHUNK
        return kchunk <= qchunk

    def split_queries(src_ref):
        for h, hs in enumerate(heads):
            qh = src_ref[0, :, hs]
            zero = jnp.zeros_like(qh)
            qz_ref[2 * h] = jnp.where(lane < HEAD_DIM, qh, zero)
            qz_ref[2 * h + 1] = jnp.where(lane >= HEAD_DIM, qh, zero)

    acc_ref[...] = jnp.zeros_like(acc_ref)
    m_ref[...] = jnp.full_like(m_ref, NEG)

    def qk(j, c, slot, mask, cols=every, keys_ref=None):
        if keys_ref is None:
            kh = k_ref[0, pl.ds(pl.multiple_of(j * tk, tk), tk), heads[c // 2]]
        else:
            kh = keys_ref[0, :, heads[c // 2]]
        st = lax.dot_general(kh, qz_ref[c, cols, :], _NT, preferred_element_type=jnp.float32)
        if mask is not None:
            st = jnp.where(mask, st, NEG)
        s_ref[slot, c, :, cols] = st
        mt_ref[slot, c, :, cols] = jnp.max(st, axis=0, keepdims=True)

    def softmax_pv(j, c, slot, cols=every):
        off = pl.multiple_of(j * tk, tk)
        m_old = m_ref[c, :, cols]
        m_new = jnp.maximum(m_old, mt_ref[slot, c, :, cols])
        alpha = jnp.exp2(m_old - m_new)
        p = jnp.exp2(s_ref[slot, c, :, cols] - m_new)
        m_ref[c, :, cols] = m_new
        h = c // 2
        vh = vt_ref[0, h * V_EXT:(h + 1) * V_EXT, pl.ds(off, tk)]
        acc_ref[c, :, cols] = alpha * acc_ref[c, :, cols] + jnp.dot(
            vh, p.astype(jnp.bfloat16), preferred_element_type=jnp.float32)

    lp = lam_ref[...]
    lam = (jnp.exp(jnp.sum(lp[0:1] * lp[1:2], axis=1, keepdims=True))
           - jnp.exp(jnp.sum(lp[2:3] * lp[3:4], axis=1, keepdims=True)) + lam_init)

    def finalize(h):
        hs = heads[h]
        a1, a2 = acc_ref[2 * h], acc_ref[2 * h + 1]
        o = (a1[:V_DIM] * (1.0 / a1[V_DIM:V_DIM + 1])
             - a2[:V_DIM] * (lam / a2[V_DIM:V_DIM + 1]))
        on = o * lax.rsqrt(jnp.mean(o * o, axis=0, keepdims=True) + SUBLN_EPS)
        y = on.T * (g_ref[...] * (1.0 - lam_init)) * gate_ref[0, :, hs].astype(jnp.float32)
        y_ref[0, :, hs] = y.astype(jnp.bfloat16)

    def stage(j, slot, has_next=True, next_mask=None, cols=every, next_cols=every, last=False):
        if last:
            split_queries(qn_ref)
        for c in range(n_chain):
            if has_next:
                qk(j + 1, c, 1 - slot, next_mask, next_cols)
            if last:
                qk(0, c, 1 - slot, None, keys_ref=kn_ref)
            softmax_pv(j, c, slot, cols)
            if last and c % 2 == 1:
                finalize(c // 2)

    @pl.when(first_step)
    def _():
        split_queries(q_ref)
        for c in range(n_chain):
            qk(0, c, 0, None)

    @pl.when(qt == 0)
    def _():
        mask = tail_mask(0)
        for c in range(n_chain):
            st = jnp.where(mask, s_ref[0, c], NEG)
            s_ref[0, c] = st
            mt_ref[0, c] = jnp.max(st, axis=0, keepdims=True)

    n_pairs = jnp.maximum(qt - 1, 0)

    def quad(i, carry):
        for t in range(4):
            stage(4 * i + t, t % 2)
        return carry

    lax.fori_loop(0, n_pairs // 2, quad, 0)

    @pl.when(n_pairs % 2 == 1)
    def _():
        stage(2 * n_pairs - 2, 0)
        stage(2 * n_pairs - 1, 1)

    def diagonal_stages():
        stage(2 * qt, 0, next_mask=tail_mask(1, upper), next_cols=upper)
        stage(2 * qt + 1, 1, has_next=False, cols=upper, last=True)

    @pl.when(qt > 0)
    def _():
        stage(2 * qt - 2, 0)
        stage(2 * qt - 1, 1, next_mask=tail_mask(0))
        diagonal_stages()

    @pl.when(qt == 0)
    def _():
        diagonal_stages()


def _attention(q, k, vt, gate, lam_params, subln_g, lam_init):
    b, s, w = q.shape
    tq, tk = ATTN_TQ, ATTN_TK
    assert tq == 2 * tk and tk % CHUNK == 0
    n_chain = 2 * N_HEADS
    f32 = jnp.float32
    nq = s // tq
    qtile = lambda bi, qi: (bi, qi, 0)
    next_b = lambda bi, qi: jnp.minimum(bi + (qi + 1) // nq, b - 1)
    return pl.pallas_call(
        functools.partial(_attn_kernel, tq=tq, tk=tk, lam_init=lam_init),
        grid=(b, nq),
        in_specs=[
            pl.BlockSpec((1, tq, w), qtile),
            pl.BlockSpec((1, tq, w), lambda bi, qi: (next_b(bi, qi), (qi + 1) % nq, 0)),
            pl.BlockSpec((1, s, w), lambda bi, qi: (bi, 0, 0)),
            pl.BlockSpec((1, tk, w), lambda bi, qi: (next_b(bi, qi), 0, 0)),
            pl.BlockSpec((1, vt.shape[1], s), lambda bi, qi: (bi, 0, 0)),
            pl.BlockSpec((1, tq, w), qtile),
            _const_spec(lam_params.shape),
            _const_spec(subln_g.shape),
        ],
        out_specs=pl.BlockSpec((1, tq, w), qtile),
        out_shape=jax.ShapeDtypeStruct((b, s, w), jnp.bfloat16),
        scratch_shapes=[
            pltpu.VMEM((2, n_chain, tk, tq), f32),
            pltpu.VMEM((2, n_chain, 1, tq), f32),
            pltpu.VMEM((n_chain, tq, LANES), jnp.bfloat16),
            pltpu.VMEM((n_chain, V_EXT, tq), f32),
            pltpu.VMEM((n_chain, 1, tq), f32),
        ],
        compiler_params=pltpu.CompilerParams(
            dimension_semantics=("arbitrary", "arbitrary"), vmem_limit_bytes=VMEM_LIMIT),
        name="diff_attention",
    )(q, q, k, k, vt, gate, lam_params, subln_g)


def _rms(h, g):
    return h * lax.rsqrt(jnp.mean(h * h, axis=-1, keepdims=True) + EPS) * g


def _out_kernel(x_ref, yc_ref, ya_ref, p_ref, wo_ref, wg_ref, wp_ref, fn_ref, o_ref):
    f32, bf16 = jnp.float32, jnp.bfloat16
    tm = x_ref.shape[0]
    halves = [slice(r0, r0 + OUT_SUB) for r0 in range(0, tm, OUT_SUB)]
    mix = [jnp.dot(yc_ref[rows, :], wo_ref[0:CONV_WIDTH, :], preferred_element_type=f32)
           + jnp.dot(ya_ref[rows, :], wo_ref[CONV_WIDTH:, :], preferred_element_type=f32)
           for rows in halves]
    h, z, pp = [], [], []
    for rows, m in zip(halves, mix):
        h.append(x_ref[rows, :] + m)
        z.append(jnp.dot(h[-1].astype(bf16), wg_ref[...], preferred_element_type=f32))
        pp.append(jnp.dot(p_ref[rows, :].astype(bf16), wp_ref[...], preferred_element_type=f32))
    for rows, hh, zz, pr in zip(halves, h, z, pp):
        inv = lax.rsqrt(jnp.mean(hh * hh, axis=-1, keepdims=True) + EPS)
        o_ref[rows, :] = _rms(hh + jax.nn.sigmoid(zz * inv) * pr, fn_ref[...])


def _output(x2, yc2, ya2, p2, wo_bf, wg_bf, wp_bf, final_norm):
    t, d = x2.shape
    tm = OUT_TM
    row = lambda i: (i, 0)
    return pl.pallas_call(
        _out_kernel,
        grid=(t // tm,),
        in_specs=[
            pl.BlockSpec((tm, d), row),
            pl.BlockSpec((tm, CONV_WIDTH), row),
            pl.BlockSpec((tm, ATTN_WIDTH), row),
            pl.BlockSpec((tm, PLE_DIM), row),
            _const_spec(wo_bf.shape),
            _const_spec(wg_bf.shape),
            _const_spec(wp_bf.shape),
            _const_spec(final_norm.shape),
        ],
        out_specs=pl.BlockSpec((tm, d), row),
        out_shape=jax.ShapeDtypeStruct((t, d), jnp.float32),
        compiler_params=pltpu.CompilerParams(
            dimension_semantics=("parallel",), vmem_limit_bytes=VMEM_LIMIT),
        name="out_ple_norm",
    )(x2, yc2, ya2, p2, wo_bf, wg_bf, wp_bf, final_norm)


def kernel(x, p, positions, norm_mix, w_in, conv_w, conv_b, lambda_q1, lambda_k1, lambda_q2,
           lambda_k2, subln_g, w_out, norm_ple, w_ple_gate, w_ple_proj, final_norm):
    b, s, d = x.shape
    depth = p.shape[0]
    assert depth == 1 and d == D_MODEL and w_in.shape[-1] == 8 * GROUP
    assert s % PROJ_TM == 0 and s % ATTN_TQ == 0 and (b * s) % OUT_TM == 0
    bf16 = jnp.bfloat16
    lam_init = 0.8 - 0.6 * math.exp(-0.3 * 0)

    half = ROT_DIM // 2
    invf = (ROPE_THETA ** (-jnp.arange(half, dtype=jnp.float32) / half))[:, None]
    yconv, q, k, vt, gate = _projection(
        x, positions[:, None, :], norm_mix[0][None, :], w_in[0],
        conv_w[0], conv_b[0][None, :], invf)

    lam_params = jnp.stack([lambda_q1[0], lambda_k1[0], lambda_q2[0], lambda_k2[0]])
    yattn = _attention(q, k, vt, gate, lam_params, subln_g[0][None, :], lam_init)

    out = _output(
        x.reshape(b * s, d), yconv.reshape(b * s, CONV_WIDTH), yattn.reshape(b * s, ATTN_WIDTH),
        p[0].reshape(b * s, PLE_DIM), w_out[0].astype(bf16),
        (norm_ple[0][:, None] * w_ple_gate[0]).astype(bf16),
        w_ple_proj[0].astype(bf16), final_norm[None, :])
    return out.reshape(b, s, d)
```

```python
import functools
import math

import jax
import jax.numpy as jnp
from jax import lax
from jax.experimental import pallas as pl
from jax.experimental.pallas import tpu as pltpu

D_MODEL = 1024
CHUNK = 64
PLE_DIM = 256
CONV_WIDTH = 512
CONV_K = 3
ATTN_WIDTH = 512
N_HEADS = 4
HEAD_DIM = 64
V_DIM = 2 * HEAD_DIM
V_EXT = V_DIM + 16
ROT_DIM = HEAD_DIM // 4
ROPE_THETA = 500000.0
EPS = 1e-6
SUBLN_EPS = 1e-5
GROUP = 512
V_GROUP = 6
LANES = 128
SUBLANES = 8
NEG = -1e30

PROJ_TM = 1024
PROJ_SUB = 256
ATTN_TQ = 512
ATTN_TK = 256
OUT_TM = 1024
OUT_SUB = 256
VMEM_LIMIT = 56 * 1024 * 1024

_NT = (((1,), (1,)), ((), ()))


def _const_spec(shape):
    return pl.BlockSpec(shape, lambda *_: (0,) * len(shape), pipeline_mode=pl.Buffered(1))


def _proj_kernel(x_ref, pos_ref, g_ref, w32_ref, cw_ref, cb_ref, invf_ref,
                 yconv_ref, q_ref, k_ref, vt_ref, gate_ref, w_ref, wvt_ref, ubuf_ref, *, tm, sub):
    si = pl.program_id(1)
    bf16 = jnp.bfloat16

    @pl.when(jnp.logical_and(pl.program_id(0) == 0, si == 0))
    def _():
        for c in range(w32_ref.shape[1] // GROUP):
            cols = slice(c * GROUP, (c + 1) * GROUP)
            w_ref[:, cols] = w32_ref[:, cols].astype(bf16)
        wvt_ref[...] = w32_ref[:, V_GROUP * GROUP:(V_GROUP + 1) * GROUP].T.astype(bf16)

    @pl.when(si == 0)
    def _():
        ubuf_ref[...] = jnp.zeros_like(ubuf_ref)

    prev = ubuf_ref[...]
    row = lax.broadcasted_iota(jnp.int32, (SUBLANES, 1), 0)
    lane = lax.broadcasted_iota(jnp.int32, (1, LANES), 1)
    low = (lane % HEAD_DIM) < (ROT_DIM // 2)
    fill = HEAD_DIM - ROT_DIM
    one, zero = jnp.ones((fill, sub), jnp.float32), jnp.zeros((fill, sub), jnp.float32)
    ones_rows = jnp.ones((V_EXT - V_DIM, sub), bf16)

    for r0 in range(0, tm, sub):
        rows = slice(r0, r0 + sub)
        x = x_ref[0, rows, :]
        ms = jnp.mean(x * x, axis=-1, keepdims=True)
        u = (x * lax.rsqrt(ms + EPS) * g_ref[...]).astype(bf16)

        def proj(c):
            return jnp.dot(u, w_ref[:, c * GROUP:(c + 1) * GROUP], preferred_element_type=jnp.float32)

        uc = proj(2) * proj(0)

        def shifted(k):
            r = pltpu.roll(uc, k, 0)
            head = jnp.where(row < k, pltpu.roll(prev, k, 0), r[0:SUBLANES, :])
            return jnp.concatenate([head, r[SUBLANES:, :]], axis=0)

        conv = (cw_ref[0:1, :] * shifted(2) + cw_ref[1:2, :] * shifted(1)
                + cw_ref[2:3, :] * uc + cb_ref[...])
        prev = uc[sub - SUBLANES:, :]
        cz = proj(3)
        yconv_ref[0, rows, :] = (proj(1) * conv * (cz * jax.nn.sigmoid(cz))).astype(bf16)

        ang = invf_ref[...] * pos_ref[0, :, rows].astype(jnp.float32)
        c8, s8 = jnp.cos(ang), jnp.sin(ang)
        cos = jnp.concatenate([c8, c8, one] * 2, axis=0).T
        sin = jnp.concatenate([-s8, s8, zero] * 2, axis=0).T

        def rope(t, scale):
            outs = []
            for h in range(N_HEADS):
                th = t[:, h * LANES:(h + 1) * LANES]
                partner = jnp.where(low, pltpu.roll(th, LANES - ROT_DIM // 2, 1),
                                    pltpu.roll(th, ROT_DIM // 2, 1))
                r = th * cos + partner * sin
                outs.append(r * scale if scale != 1.0 else r)
            return jnp.concatenate(outs, axis=1)

        q_ref[0, rows, :] = rope(proj(4), HEAD_DIM ** -0.5 * math.log2(math.e)).astype(bf16)
        k_ref[0, rows, :] = rope(proj(5), 1.0).astype(bf16)
        az = proj(7)
        gate_ref[0, rows, :] = (az * jax.nn.sigmoid(az)).astype(bf16)
        vt = lax.dot_general(wvt_ref[...], u, _NT, preferred_element_type=jnp.float32).astype(bf16)
        for h in range(N_HEADS):
            vt_ref[0, h * V_EXT:h * V_EXT + V_DIM, rows] = vt[h * V_DIM:(h + 1) * V_DIM, :]
            vt_ref[0, h * V_EXT + V_DIM:(h + 1) * V_EXT, rows] = ones_rows

    ubuf_ref[...] = prev


def _projection(x, pos3, g, w32, cw, cb, invf):
    b, s, d = x.shape
    tm = PROJ_TM
    tok = lambda bi, si: (bi, si, 0)
    out_tok = jax.ShapeDtypeStruct((b, s, GROUP), jnp.bfloat16)
    vt_rows = N_HEADS * V_EXT
    return pl.pallas_call(
        functools.partial(_proj_kernel, tm=tm, sub=PROJ_SUB),
        grid=(b, s // tm),
        in_specs=[
            pl.BlockSpec((1, tm, d), tok),
            pl.BlockSpec((1, 1, tm), lambda bi, si: (bi, 0, si)),
            _const_spec((1, d)),
            _const_spec(w32.shape),
            _const_spec(cw.shape),
            _const_spec(cb.shape),
            _const_spec(invf.shape),
        ],
        out_specs=[
            pl.BlockSpec((1, tm, GROUP), tok),
            pl.BlockSpec((1, tm, GROUP), tok),
            pl.BlockSpec((1, tm, GROUP), tok),
            pl.BlockSpec((1, vt_rows, tm), lambda bi, si: (bi, 0, si)),
            pl.BlockSpec((1, tm, GROUP), tok),
        ],
        out_shape=[out_tok, out_tok, out_tok,
                   jax.ShapeDtypeStruct((b, vt_rows, s), jnp.bfloat16), out_tok],
        scratch_shapes=[
            pltpu.VMEM(w32.shape, jnp.bfloat16),
            pltpu.VMEM((GROUP, d), jnp.bfloat16),
            pltpu.VMEM((SUBLANES, GROUP), jnp.float32),
        ],
        compiler_params=pltpu.CompilerParams(
            dimension_semantics=("arbitrary", "arbitrary"), vmem_limit_bytes=VMEM_LIMIT),
        name="proj_conv_rope",
    )(x, pos3, g, w32, cw, cb, invf)


def _attn_kernel(q_ref, qn_ref, k_ref, kn_ref, vt_ref, gate_ref, lam_ref, g_ref, y_ref,
                 s_ref, mt_ref, qz_ref, acc_ref, m_ref, *, tq, tk, lam_init):
    qt = pl.program_id(1)
    first_step = jnp.logical_and(pl.program_id(0) == 0, qt == 0)
    n_chain = 2 * N_HEADS
    heads = [slice(h * LANES, (h + 1) * LANES) for h in range(N_HEADS)]
    lane = lax.broadcasted_iota(jnp.int32, (1, LANES), 1)
    every = slice(0, tq)
    upper = slice(tq // 2, tq)

    def tail_mask(key_tile, cols=every):
        n = cols.stop - cols.start
        kchunk = lax.broadcasted_iota(jnp.int32, (tk, n), 0) // CHUNK + key_tile * (tk // CHUNK)
        qchunk = (lax.broadcasted_iota(jnp.int32, (tk, n), 1) + cols.start) // CHUNK
        return kchunk <= qchunk

    def split_queries(src_ref):
        for h, hs in enumerate(heads):
            qh = src_ref[0, :, hs]
            zero = jnp.zeros_like(qh)
            qz_ref[2 * h] = jnp.where(lane < HEAD_DIM, qh, zero)
            qz_ref[2 * h + 1] = jnp.where(lane >= HEAD_DIM, qh, zero)

    acc_ref[...] = jnp.zeros_like(acc_ref)
    m_ref[...] = jnp.full_like(m_ref, NEG)

    def qk(j, c, slot, mask, cols=every, keys_ref=None):
        if keys_ref is None:
            kh = k_ref[0, pl.ds(pl.multiple_of(j * tk, tk), tk), heads[c // 2]]
        else:
            kh = keys_ref[0, :, heads[c // 2]]
        st = lax.dot_general(kh, qz_ref[c, cols, :], _NT, preferred_element_type=jnp.float32)
        if mask is not None:
            st = jnp.where(mask, st, NEG)
        s_ref[slot, c, :, cols] = st
        mt_ref[slot, c, :, cols] = jnp.max(st, axis=0, keepdims=True)

    def softmax_pv(j, c, slot, cols=every):
        off = pl.multiple_of(j * tk, tk)
        m_old = m_ref[c, :, cols]
        m_new = jnp.maximum(m_old, mt_ref[slot, c, :, cols])
        alpha = jnp.exp2(m_old - m_new)
        p = jnp.exp2(s_ref[slot, c, :, cols] - m_new)
        m_ref[c, :, cols] = m_new
        h = c // 2
        vh = vt_ref[0, h * V_EXT:(h + 1) * V_EXT, pl.ds(off, tk)]
        acc_ref[c, :, cols] = alpha * acc_ref[c, :, cols] + jnp.dot(
            vh, p.astype(jnp.bfloat16), preferred_element_type=jnp.float32)

    lp = lam_ref[...]
    lam = (jnp.exp(jnp.sum(lp[0:1] * lp[1:2], axis=1, keepdims=True))
           - jnp.exp(jnp.sum(lp[2:3] * lp[3:4], axis=1, keepdims=True)) + lam_init)

    def finalize(h):
        hs = heads[h]
        a1, a2 = acc_ref[2 * h], acc_ref[2 * h + 1]
        o = (a1[:V_DIM] * (1.0 / a1[V_DIM:V_DIM + 1])
             - a2[:V_DIM] * (lam / a2[V_DIM:V_DIM + 1]))
        on = o * lax.rsqrt(jnp.mean(o * o, axis=0, keepdims=True) + SUBLN_EPS)
        y = on.T * (g_ref[...] * (1.0 - lam_init)) * gate_ref[0, :, hs].astype(jnp.float32)
        y_ref[0, :, hs] = y.astype(jnp.bfloat16)

    def visible_stage(j, slot):
        for c in range(n_chain):
            for half in (slice(0, tq // 2), upper):
                qk(j + 1, c, 1 - slot, None, half)
                softmax_pv(j, c, slot, half)

    def stage(j, slot, has_next=True, next_mask=None, cols=every, next_cols=every, last=False):
        if last:
            split_queries(qn_ref)
        for c in range(n_chain):
            if has_next:
                qk(j + 1, c, 1 - slot, next_mask, next_cols)
            if last:
                qk(0, c, 1 - slot, None, keys_ref=kn_ref)
            softmax_pv(j, c, slot, cols)
            if last and c % 2 == 1:
                finalize(c // 2)

    @pl.when(first_step)
    def _():
        split_queries(q_ref)
        for c in range(n_chain):
            qk(0, c, 0, None)

    @pl.when(qt == 0)
    def _():
        mask = tail_mask(0)
        for c in range(n_chain):
            st = jnp.where(mask, s_ref[0, c], NEG)
            s_ref[0, c] = st
            mt_ref[0, c] = jnp.max(st, axis=0, keepdims=True)

    n_pairs = jnp.maximum(qt - 1, 0)

    def quad(i, carry):
        for t in range(4):
            visible_stage(4 * i + t, t % 2)
        return carry

    lax.fori_loop(0, n_pairs // 2, quad, 0)

    @pl.when(n_pairs % 2 == 1)
    def _():
        visible_stage(2 * n_pairs - 2, 0)
        visible_stage(2 * n_pairs - 1, 1)

    def diagonal_stages():
        stage(2 * qt, 0, next_mask=tail_mask(1, upper), next_cols=upper)
        stage(2 * qt + 1, 1, has_next=False, cols=upper, last=True)

    @pl.when(qt > 0)
    def _():
        stage(2 * qt - 2, 0)
        stage(2 * qt - 1, 1, next_mask=tail_mask(0))
        diagonal_stages()

    @pl.when(qt == 0)
    def _():
        diagonal_stages()


def _attention(q, k, vt, gate, lam_params, subln_g, lam_init):
    b, s, w = q.shape
    tq, tk = ATTN_TQ, ATTN_TK
    assert tq == 2 * tk and tk % CHUNK == 0
    n_chain = 2 * N_HEADS
    f32 = jnp.float32
    nq = s // tq
    qtile = lambda bi, qi: (bi, qi, 0)
    next_b = lambda bi, qi: jnp.minimum(bi + (qi + 1) // nq, b - 1)
    return pl.pallas_call(
        functools.partial(_attn_kernel, tq=tq, tk=tk, lam_init=lam_init),
        grid=(b, nq),
        in_specs=[
            pl.BlockSpec((1, tq, w), qtile),
            pl.BlockSpec((1, tq, w), lambda bi, qi: (next_b(bi, qi), (qi + 1) % nq, 0)),
            pl.BlockSpec((1, s, w), lambda bi, qi: (bi, 0, 0)),
            pl.BlockSpec((1, tk, w), lambda bi, qi: (next_b(bi, qi), 0, 0)),
            pl.BlockSpec((1, vt.shape[1], s), lambda bi, qi: (bi, 0, 0)),
            pl.BlockSpec((1, tq, w), qtile),
            _const_spec(lam_params.shape),
            _const_spec(subln_g.shape),
        ],
        out_specs=pl.BlockSpec((1, tq, w), qtile),
        out_shape=jax.ShapeDtypeStruct((b, s, w), jnp.bfloat16),
        scratch_shapes=[
            pltpu.VMEM((2, n_chain, tk, tq), f32),
            pltpu.VMEM((2, n_chain, 1, tq), f32),
            pltpu.VMEM((n_chain, tq, LANES), jnp.bfloat16),
            pltpu.VMEM((n_chain, V_EXT, tq), f32),
            pltpu.VMEM((n_chain, 1, tq), f32),
        ],
        compiler_params=pltpu.CompilerParams(
            dimension_semantics=("arbitrary", "arbitrary"), vmem_limit_bytes=VMEM_LIMIT),
        name="diff_attention",
    )(q, q, k, k, vt, gate, lam_params, subln_g)


def _rms(h, g):
    return h * lax.rsqrt(jnp.mean(h * h, axis=-1, keepdims=True) + EPS) * g


def _out_kernel(x_ref, yc_ref, ya_ref, p_ref, wo_ref, wg_ref, wp_ref, fn_ref, o_ref):
    f32, bf16 = jnp.float32, jnp.bfloat16
    tm = x_ref.shape[0]
    halves = [slice(r0, r0 + OUT_SUB) for r0 in range(0, tm, OUT_SUB)]
    mix = [jnp.dot(yc_ref[rows, :], wo_ref[0:CONV_WIDTH, :], preferred_element_type=f32)
           + jnp.dot(ya_ref[rows, :], wo_ref[CONV_WIDTH:, :], preferred_element_type=f32)
           for rows in halves]
    h, z, pp = [], [], []
    for rows, m in zip(halves, mix):
        h.append(x_ref[rows, :] + m)
        z.append(jnp.dot(h[-1].astype(bf16), wg_ref[...], preferred_element_type=f32))
        pp.append(jnp.dot(p_ref[rows, :].astype(bf16), wp_ref[...], preferred_element_type=f32))
    for rows, hh, zz, pr in zip(halves, h, z, pp):
        inv = lax.rsqrt(jnp.mean(hh * hh, axis=-1, keepdims=True) + EPS)
        o_ref[rows, :] = _rms(hh + jax.nn.sigmoid(zz * inv) * pr, fn_ref[...])


def _output(x2, yc2, ya2, p2, wo_bf, wg_bf, wp_bf, final_norm):
    t, d = x2.shape
    tm = OUT_TM
    row = lambda i: (i, 0)
    return pl.pallas_call(
        _out_kernel,
        grid=(t // tm,),
        in_specs=[
            pl.BlockSpec((tm, d), row),
            pl.BlockSpec((tm, CONV_WIDTH), row),
            pl.BlockSpec((tm, ATTN_WIDTH), row),
            pl.BlockSpec((tm, PLE_DIM), row),
            _const_spec(wo_bf.shape),
            _const_spec(wg_bf.shape),
            _const_spec(wp_bf.shape),
            _const_spec(final_norm.shape),
        ],
        out_specs=pl.BlockSpec((tm, d), row),
        out_shape=jax.ShapeDtypeStruct((t, d), jnp.float32),
        compiler_params=pltpu.CompilerParams(
            dimension_semantics=("parallel",), vmem_limit_bytes=VMEM_LIMIT),
        name="out_ple_norm",
    )(x2, yc2, ya2, p2, wo_bf, wg_bf, wp_bf, final_norm)


def kernel(x, p, positions, norm_mix, w_in, conv_w, conv_b, lambda_q1, lambda_k1, lambda_q2,
           lambda_k2, subln_g, w_out, norm_ple, w_ple_gate, w_ple_proj, final_norm):
    b, s, d = x.shape
    depth = p.shape[0]
    assert depth == 1 and d == D_MODEL and w_in.shape[-1] == 8 * GROUP
    assert s % PROJ_TM == 0 and s % ATTN_TQ == 0 and (b * s) % OUT_TM == 0
    bf16 = jnp.bfloat16
    lam_init = 0.8 - 0.6 * math.exp(-0.3 * 0)

    half = ROT_DIM // 2
    invf = (ROPE_THETA ** (-jnp.arange(half, dtype=jnp.float32) / half))[:, None]
    yconv, q, k, vt, gate = _projection(
        x, positions[:, None, :], norm_mix[0][None, :], w_in[0],
        conv_w[0], conv_b[0][None, :], invf)

    lam_params = jnp.stack([lambda_q1[0], lambda_k1[0], lambda_q2[0], lambda_k2[0]])
    yattn = _attention(q, k, vt, gate, lam_params, subln_g[0][None, :], lam_init)

    out = _output(
        x.reshape(b * s, d), yconv.reshape(b * s, CONV_WIDTH), yattn.reshape(b * s, ATTN_WIDTH),
        p[0].reshape(b * s, PLE_DIM), w_out[0].astype(bf16),
        (norm_ple[0][:, None] * w_ple_gate[0]).astype(bf16),
        w_ple_proj[0].astype(bf16), final_norm[None, :])
    return out.reshape(b, s, d)
```

```python
import functools
import math

import jax
import jax.numpy as jnp
from jax import lax
from jax.experimental import pallas as pl
from jax.experimental.pallas import tpu as pltpu

D_MODEL = 1024
CHUNK = 64
PLE_DIM = 256
CONV_WIDTH = 512
CONV_K = 3
ATTN_WIDTH = 512
N_HEADS = 4
HEAD_DIM = 64
V_DIM = 2 * HEAD_DIM
V_EXT = V_DIM + 16
ROT_DIM = HEAD_DIM // 4
ROPE_THETA = 500000.0
EPS = 1e-6
SUBLN_EPS = 1e-5
GROUP = 512
V_GROUP = 6
LANES = 128
SUBLANES = 8
NEG = -1e30

PROJ_TM = 1024
PROJ_SUB = 256
ATTN_TQ = 512
ATTN_TK = 256
OUT_TM = 1024
OUT_SUB = 256
VMEM_LIMIT = 56 * 1024 * 1024

_NT = (((1,), (1,)), ((), ()))


def _const_spec(shape):
    return pl.BlockSpec(shape, lambda *_: (0,) * len(shape), pipeline_mode=pl.Buffered(1))


def _proj_kernel(x_ref, pos_ref, g_ref, w32_ref, cw_ref, cb_ref, invf_ref,
                 yconv_ref, q_ref, k_ref, vt_ref, gate_ref, w_ref, wvt_ref, ubuf_ref, *, tm, sub):
    si = pl.program_id(1)
    bf16 = jnp.bfloat16

    @pl.when(jnp.logical_and(pl.program_id(0) == 0, si == 0))
    def _():
        for c in range(w32_ref.shape[1] // GROUP):
            cols = slice(c * GROUP, (c + 1) * GROUP)
            w_ref[:, cols] = w32_ref[:, cols].astype(bf16)
        wvt_ref[...] = w32_ref[:, V_GROUP * GROUP:(V_GROUP + 1) * GROUP].T.astype(bf16)

    @pl.when(si == 0)
    def _():
        ubuf_ref[...] = jnp.zeros_like(ubuf_ref)

    prev = ubuf_ref[...]
    row = lax.broadcasted_iota(jnp.int32, (SUBLANES, 1), 0)
    lane = lax.broadcasted_iota(jnp.int32, (1, LANES), 1)
    low = (lane % HEAD_DIM) < (ROT_DIM // 2)
    fill = HEAD_DIM - ROT_DIM
    one, zero = jnp.ones((fill, sub), jnp.float32), jnp.zeros((fill, sub), jnp.float32)
    ones_rows = jnp.ones((V_EXT - V_DIM, sub), bf16)

    for r0 in range(0, tm, sub):
        rows = slice(r0, r0 + sub)
        x = x_ref[0, rows, :]
        ms = jnp.mean(x * x, axis=-1, keepdims=True)
        u = (x * lax.rsqrt(ms + EPS) * g_ref[...]).astype(bf16)

        def proj(c):
            return jnp.dot(u, w_ref[:, c * GROUP:(c + 1) * GROUP], preferred_element_type=jnp.float32)

        uc = proj(2) * proj(0)

        def shifted(k):
            r = pltpu.roll(uc, k, 0)
            head = jnp.where(row < k, pltpu.roll(prev, k, 0), r[0:SUBLANES, :])
            return jnp.concatenate([head, r[SUBLANES:, :]], axis=0)

        conv = (cw_ref[0:1, :] * shifted(2) + cw_ref[1:2, :] * shifted(1)
                + cw_ref[2:3, :] * uc + cb_ref[...])
        prev = uc[sub - SUBLANES:, :]
        cz = proj(3)
        yconv_ref[0, rows, :] = (proj(1) * conv * (cz * jax.nn.sigmoid(cz))).astype(bf16)

        ang = invf_ref[...] * pos_ref[0, :, rows].astype(jnp.float32)
        c8, s8 = jnp.cos(ang), jnp.sin(ang)
        cos = jnp.concatenate([c8, c8, one] * 2, axis=0).T
        sin = jnp.concatenate([-s8, s8, zero] * 2, axis=0).T

        def rope(t, scale):
            outs = []
            for h in range(N_HEADS):
                th = t[:, h * LANES:(h + 1) * LANES]
                partner = jnp.where(low, pltpu.roll(th, LANES - ROT_DIM // 2, 1),
                                    pltpu.roll(th, ROT_DIM // 2, 1))
                r = th * cos + partner * sin
                outs.append(r * scale if scale != 1.0 else r)
            return jnp.concatenate(outs, axis=1)

        q_ref[0, rows, :] = rope(proj(4), HEAD_DIM ** -0.5 * math.log2(math.e)).astype(bf16)
        k_ref[0, rows, :] = rope(proj(5), 1.0).astype(bf16)
        az = proj(7)
        gate_ref[0, rows, :] = (az * jax.nn.sigmoid(az)).astype(bf16)
        vt = lax.dot_general(wvt_ref[...], u, _NT, preferred_element_type=jnp.float32).astype(bf16)
        for h in range(N_HEADS):
            vt_ref[0, r0 // sub, h * V_EXT:h * V_EXT + V_DIM, :] = vt[h * V_DIM:(h + 1) * V_DIM, :]
            vt_ref[0, r0 // sub, h * V_EXT + V_DIM:(h + 1) * V_EXT, :] = ones_rows

    ubuf_ref[...] = prev


def _projection(x, pos3, g, w32, cw, cb, invf):
    b, s, d = x.shape
    tm = PROJ_TM
    tok = lambda bi, si: (bi, si, 0)
    out_tok = jax.ShapeDtypeStruct((b, s, GROUP), jnp.bfloat16)
    vt_rows = N_HEADS * V_EXT
    return pl.pallas_call(
        functools.partial(_proj_kernel, tm=tm, sub=PROJ_SUB),
        grid=(b, s // tm),
        in_specs=[
            pl.BlockSpec((1, tm, d), tok),
            pl.BlockSpec((1, 1, tm), lambda bi, si: (bi, 0, si)),
            _const_spec((1, d)),
            _const_spec(w32.shape),
            _const_spec(cw.shape),
            _const_spec(cb.shape),
            _const_spec(invf.shape),
        ],
        out_specs=[
            pl.BlockSpec((1, tm, GROUP), tok),
            pl.BlockSpec((1, tm, GROUP), tok),
            pl.BlockSpec((1, tm, GROUP), tok),
            pl.BlockSpec((1, tm // PROJ_SUB, vt_rows, PROJ_SUB), lambda bi, si: (bi, si, 0, 0)),
            pl.BlockSpec((1, tm, GROUP), tok),
        ],
        out_shape=[out_tok, out_tok, out_tok,
                   jax.ShapeDtypeStruct((b, s // PROJ_SUB, vt_rows, PROJ_SUB), jnp.bfloat16), out_tok],
        scratch_shapes=[
            pltpu.VMEM(w32.shape, jnp.bfloat16),
            pltpu.VMEM((GROUP, d), jnp.bfloat16),
            pltpu.VMEM((SUBLANES, GROUP), jnp.float32),
        ],
        compiler_params=pltpu.CompilerParams(
            dimension_semantics=("arbitrary", "arbitrary"), vmem_limit_bytes=VMEM_LIMIT),
        name="proj_conv_rope",
    )(x, pos3, g, w32, cw, cb, invf)


def _attn_kernel(q_ref, qn_ref, k_ref, kn_ref, vt_ref, gate_ref, lam_ref, g_ref, y_ref,
                 s_ref, mt_ref, qz_ref, acc_ref, m_ref, *, tq, tk, lam_init):
    qt = pl.program_id(1)
    first_step = jnp.logical_and(pl.program_id(0) == 0, qt == 0)
    n_chain = 2 * N_HEADS
    heads = [slice(h * LANES, (h + 1) * LANES) for h in range(N_HEADS)]
    lane = lax.broadcasted_iota(jnp.int32, (1, LANES), 1)
    every = slice(0, tq)
    upper = slice(tq // 2, tq)

    def tail_mask(key_tile, cols=every):
        n = cols.stop - cols.start
        kchunk = lax.broadcasted_iota(jnp.int32, (tk, n), 0) // CHUNK + key_tile * (tk // CHUNK)
        qchunk = (lax.broadcasted_iota(jnp.int32, (tk, n), 1) + cols.start) // CHUNK
        return kchunk <= qchunk

    def split_queries(src_ref):
        for h, hs in enumerate(heads):
            qh = src_ref[0, :, hs]
            zero = jnp.zeros_like(qh)
            qz_ref[2 * h] = jnp.where(lane < HEAD_DIM, qh, zero)
            qz_ref[2 * h + 1] = jnp.where(lane >= HEAD_DIM, qh, zero)

    acc_ref[...] = jnp.zeros_like(acc_ref)
    m_ref[...] = jnp.full_like(m_ref, NEG)

    def qk(j, c, slot, mask, cols=every, keys_ref=None):
        if keys_ref is None:
            kh = k_ref[0, pl.ds(pl.multiple_of(j * tk, tk), tk), heads[c // 2]]
        else:
            kh = keys_ref[0, :, heads[c // 2]]
        st = lax.dot_general(kh, qz_ref[c, cols, :], _NT, preferred_element_type=jnp.float32)
        if mask is not None:
            st = jnp.where(mask, st, NEG)
        s_ref[slot, c, :, cols] = st
        mt_ref[slot, c, :, cols] = jnp.max(st, axis=0, keepdims=True)

    def softmax_pv(j, c, slot, cols=every):
        m_old = m_ref[c, :, cols]
        m_new = jnp.maximum(m_old, mt_ref[slot, c, :, cols])
        alpha = jnp.exp2(m_old - m_new)
        p = jnp.exp2(s_ref[slot, c, :, cols] - m_new)
        m_ref[c, :, cols] = m_new
        h = c // 2
        vh = vt_ref[0, j, h * V_EXT:(h + 1) * V_EXT, :]
        acc_ref[c, :, cols] = alpha * acc_ref[c, :, cols] + jnp.dot(
            vh, p.astype(jnp.bfloat16), preferred_element_type=jnp.float32)

    lp = lam_ref[...]
    lam = (jnp.exp(jnp.sum(lp[0:1] * lp[1:2], axis=1, keepdims=True))
           - jnp.exp(jnp.sum(lp[2:3] * lp[3:4], axis=1, keepdims=True)) + lam_init)

    def finalize(h):
        hs = heads[h]
        a1, a2 = acc_ref[2 * h], acc_ref[2 * h + 1]
        o = (a1[:V_DIM] * (1.0 / a1[V_DIM:V_DIM + 1])
             - a2[:V_DIM] * (lam / a2[V_DIM:V_DIM + 1]))
        on = o * lax.rsqrt(jnp.mean(o * o, axis=0, keepdims=True) + SUBLN_EPS)
        y = on.T * (g_ref[...] * (1.0 - lam_init)) * gate_ref[0, :, hs].astype(jnp.float32)
        y_ref[0, :, hs] = y.astype(jnp.bfloat16)

    def visible_stage(j, slot):
        for c in range(n_chain):
            for half in (slice(0, tq // 2), upper):
                qk(j + 1, c, 1 - slot, None, half)
                softmax_pv(j, c, slot, half)

    def stage(j, slot, has_next=True, next_mask=None, cols=every, next_cols=every, last=False):
        if last:
            split_queries(qn_ref)
        for c in range(n_chain):
            if has_next:
                qk(j + 1, c, 1 - slot, next_mask, next_cols)
            if last:
                qk(0, c, 1 - slot, None, keys_ref=kn_ref)
            softmax_pv(j, c, slot, cols)
            if last and c % 2 == 1:
                finalize(c // 2)

    @pl.when(first_step)
    def _():
        split_queries(q_ref)
        for c in range(n_chain):
            qk(0, c, 0, None)

    @pl.when(qt == 0)
    def _():
        mask = tail_mask(0)
        for c in range(n_chain):
            st = jnp.where(mask, s_ref[0, c], NEG)
            s_ref[0, c] = st
            mt_ref[0, c] = jnp.max(st, axis=0, keepdims=True)

    n_pairs = jnp.maximum(qt - 1, 0)

    def quad(i, carry):
        for t in range(4):
            visible_stage(4 * i + t, t % 2)
        return carry

    lax.fori_loop(0, n_pairs // 2, quad, 0)

    @pl.when(n_pairs % 2 == 1)
    def _():
        visible_stage(2 * n_pairs - 2, 0)
        visible_stage(2 * n_pairs - 1, 1)

    def diagonal_stages():
        stage(2 * qt, 0, next_mask=tail_mask(1, upper), next_cols=upper)
        stage(2 * qt + 1, 1, has_next=False, cols=upper, last=True)

    @pl.when(qt > 0)
    def _():
        stage(2 * qt - 2, 0)
        stage(2 * qt - 1, 1, next_mask=tail_mask(0))
        diagonal_stages()

    @pl.when(qt == 0)
    def _():
        diagonal_stages()


def _attention(q, k, vt, gate, lam_params, subln_g, lam_init):
    b, s, w = q.shape
    tq, tk = ATTN_TQ, ATTN_TK
    assert tq == 2 * tk and tk % CHUNK == 0 and vt.shape[3] == tk
    n_chain = 2 * N_HEADS
    f32 = jnp.float32
    nq = s // tq
    qtile = lambda bi, qi: (bi, qi, 0)
    next_b = lambda bi, qi: jnp.minimum(bi + (qi + 1) // nq, b - 1)
    return pl.pallas_call(
        functools.partial(_attn_kernel, tq=tq, tk=tk, lam_init=lam_init),
        grid=(b, nq),
        in_specs=[
            pl.BlockSpec((1, tq, w), qtile),
            pl.BlockSpec((1, tq, w), lambda bi, qi: (next_b(bi, qi), (qi + 1) % nq, 0)),
            pl.BlockSpec((1, s, w), lambda bi, qi: (bi, 0, 0)),
            pl.BlockSpec((1, tk, w), lambda bi, qi: (next_b(bi, qi), 0, 0)),
            pl.BlockSpec((1,) + vt.shape[1:], lambda bi, qi: (bi, 0, 0, 0)),
            pl.BlockSpec((1, tq, w), qtile),
            _const_spec(lam_params.shape),
            _const_spec(subln_g.shape),
        ],
        out_specs=pl.BlockSpec((1, tq, w), qtile),
        out_shape=jax.ShapeDtypeStruct((b, s, w), jnp.bfloat16),
        scratch_shapes=[
            pltpu.VMEM((2, n_chain, tk, tq), f32),
            pltpu.VMEM((2, n_chain, 1, tq), f32),
            pltpu.VMEM((n_chain, tq, LANES), jnp.bfloat16),
            pltpu.VMEM((n_chain, V_EXT, tq), f32),
            pltpu.VMEM((n_chain, 1, tq), f32),
        ],
        compiler_params=pltpu.CompilerParams(
            dimension_semantics=("arbitrary", "arbitrary"), vmem_limit_bytes=VMEM_LIMIT),
        name="diff_attention",
    )(q, q, k, k, vt, gate, lam_params, subln_g)


def _rms(h, g):
    return h * lax.rsqrt(jnp.mean(h * h, axis=-1, keepdims=True) + EPS) * g


def _out_kernel(x_ref, yc_ref, ya_ref, p_ref, wo_ref, wg_ref, wp_ref, fn_ref, o_ref):
    f32, bf16 = jnp.float32, jnp.bfloat16
    tm = x_ref.shape[0]
    halves = [slice(r0, r0 + OUT_SUB) for r0 in range(0, tm, OUT_SUB)]
    mix = [jnp.dot(yc_ref[rows, :], wo_ref[0:CONV_WIDTH, :], preferred_element_type=f32)
           + jnp.dot(ya_ref[rows, :], wo_ref[CONV_WIDTH:, :], preferred_element_type=f32)
           for rows in halves]
    h, z, pp = [], [], []
    for rows, m in zip(halves, mix):
        h.append(x_ref[rows, :] + m)
        z.append(jnp.dot(h[-1].astype(bf16), wg_ref[...], preferred_element_type=f32))
        pp.append(jnp.dot(p_ref[rows, :].astype(bf16), wp_ref[...], preferred_element_type=f32))
    for rows, hh, zz, pr in zip(halves, h, z, pp):
        inv = lax.rsqrt(jnp.mean(hh * hh, axis=-1, keepdims=True) + EPS)
        o_ref[rows, :] = _rms(hh + jax.nn.sigmoid(zz * inv) * pr, fn_ref[...])


def _output(x2, yc2, ya2, p2, wo_bf, wg_bf, wp_bf, final_norm):
    t, d = x2.shape
    tm = OUT_TM
    row = lambda i: (i, 0)
    return pl.pallas_call(
        _out_kernel,
        grid=(t // tm,),
        in_specs=[
            pl.BlockSpec((tm, d), row),
            pl.BlockSpec((tm, CONV_WIDTH), row),
            pl.BlockSpec((tm, ATTN_WIDTH), row),
            pl.BlockSpec((tm, PLE_DIM), row),
            _const_spec(wo_bf.shape),
            _const_spec(wg_bf.shape),
            _const_spec(wp_bf.shape),
            _const_spec(final_norm.shape),
        ],
        out_specs=pl.BlockSpec((tm, d), row),
        out_shape=jax.ShapeDtypeStruct((t, d), jnp.float32),
        compiler_params=pltpu.CompilerParams(
            dimension_semantics=("parallel",), vmem_limit_bytes=VMEM_LIMIT),
        name="out_ple_norm",
    )(x2, yc2, ya2, p2, wo_bf, wg_bf, wp_bf, final_norm)


def kernel(x, p, positions, norm_mix, w_in, conv_w, conv_b, lambda_q1, lambda_k1, lambda_q2,
           lambda_k2, subln_g, w_out, norm_ple, w_ple_gate, w_ple_proj, final_norm):
    b, s, d = x.shape
    depth = p.shape[0]
    assert depth == 1 and d == D_MODEL and w_in.shape[-1] == 8 * GROUP
    assert s % PROJ_TM == 0 and s % ATTN_TQ == 0 and (b * s) % OUT_TM == 0
    bf16 = jnp.bfloat16
    lam_init = 0.8 - 0.6 * math.exp(-0.3 * 0)

    half = ROT_DIM // 2
    invf = (ROPE_THETA ** (-jnp.arange(half, dtype=jnp.float32) / half))[:, None]
    yconv, q, k, vt, gate = _projection(
        x, positions[:, None, :], norm_mix[0][None, :], w_in[0],
        conv_w[0], conv_b[0][None, :], invf)

    lam_params = jnp.stack([lambda_q1[0], lambda_k1[0], lambda_q2[0], lambda_k2[0]])
    yattn = _attention(q, k, vt, gate, lam_params, subln_g[0][None, :], lam_init)

    out = _output(
        x.reshape(b * s, d), yconv.reshape(b * s, CONV_WIDTH), yattn.reshape(b * s, ATTN_WIDTH),
        p[0].reshape(b * s, PLE_DIM), w_out[0].astype(bf16),
        (norm_ple[0][:, None] * w_ple_gate[0]).astype(bf16),
        w_ple_proj[0].astype(bf16), final_norm[None, :])
    return out.reshape(b, s, d)
```

```python
import functools
import math

import jax
import jax.numpy as jnp
from jax import lax
from jax.experimental import pallas as pl
from jax.experimental.pallas import tpu as pltpu

D_MODEL = 1024
CHUNK = 64
PLE_DIM = 256
CONV_WIDTH = 512
CONV_K = 3
ATTN_WIDTH = 512
N_HEADS = 4
HEAD_DIM = 64
V_DIM = 2 * HEAD_DIM
V_EXT = V_DIM + 16
ROT_DIM = HEAD_DIM // 4
ROPE_THETA = 500000.0
EPS = 1e-6
SUBLN_EPS = 1e-5
GROUP = 512
V_GROUP = 6
LANES = 128
SUBLANES = 8
NEG = -1e30

PROJ_TM = 1024
PROJ_SUB = 256
ATTN_TQ = 512
ATTN_TK = 256
OUT_TM = 1024
OUT_SUB = 256
VMEM_LIMIT = 56 * 1024 * 1024

_NT = (((1,), (1,)), ((), ()))


def _const_spec(shape):
    return pl.BlockSpec(shape, lambda *_: (0,) * len(shape), pipeline_mode=pl.Buffered(1))


def _proj_kernel(x_ref, pos_ref, g_ref, w32_ref, cw_ref, cb_ref, invf_ref,
                 yconv_ref, q_ref, k_ref, vt_ref, gate_ref, w_ref, wvt_ref, ubuf_ref, *, tm, sub):
    si = pl.program_id(1)
    bf16 = jnp.bfloat16

    @pl.when(jnp.logical_and(pl.program_id(0) == 0, si == 0))
    def _():
        for c in range(w32_ref.shape[1] // GROUP):
            cols = slice(c * GROUP, (c + 1) * GROUP)
            w_ref[:, cols] = w32_ref[:, cols].astype(bf16)
        wvt_ref[...] = w32_ref[:, V_GROUP * GROUP:(V_GROUP + 1) * GROUP].T.astype(bf16)

    @pl.when(si == 0)
    def _():
        ubuf_ref[...] = jnp.zeros_like(ubuf_ref)

    prev = ubuf_ref[...]
    row = lax.broadcasted_iota(jnp.int32, (SUBLANES, 1), 0)
    lane = lax.broadcasted_iota(jnp.int32, (1, LANES), 1)
    low = (lane % HEAD_DIM) < (ROT_DIM // 2)
    fill = HEAD_DIM - ROT_DIM
    one, zero = jnp.ones((fill, sub), jnp.float32), jnp.zeros((fill, sub), jnp.float32)
    ones_rows = jnp.ones((V_EXT - V_DIM, sub), bf16)

    for r0 in range(0, tm, sub):
        rows = slice(r0, r0 + sub)
        x = x_ref[0, rows, :]
        ms = jnp.mean(x * x, axis=-1, keepdims=True)
        u = (x * lax.rsqrt(ms + EPS) * g_ref[...]).astype(bf16)

        def proj(c):
            return jnp.dot(u, w_ref[:, c * GROUP:(c + 1) * GROUP], preferred_element_type=jnp.float32)

        uc = proj(2) * proj(0)

        def shifted(k):
            r = pltpu.roll(uc, k, 0)
            head = jnp.where(row < k, pltpu.roll(prev, k, 0), r[0:SUBLANES, :])
            return jnp.concatenate([head, r[SUBLANES:, :]], axis=0)

        conv = (cw_ref[0:1, :] * shifted(2) + cw_ref[1:2, :] * shifted(1)
                + cw_ref[2:3, :] * uc + cb_ref[...])
        prev = uc[sub - SUBLANES:, :]
        cz = proj(3)
        yconv_ref[0, rows, :] = (proj(1) * conv * (cz * jax.nn.sigmoid(cz))).astype(bf16)

        ang = invf_ref[...] * pos_ref[0, :, rows].astype(jnp.float32)
        c8, s8 = jnp.cos(ang), jnp.sin(ang)
        cos = jnp.concatenate([c8, c8, one] * 2, axis=0).T
        sin = jnp.concatenate([-s8, s8, zero] * 2, axis=0).T

        def rope(t, scale):
            outs = []
            for h in range(N_HEADS):
                th = t[:, h * LANES:(h + 1) * LANES]
                partner = jnp.where(low, pltpu.roll(th, LANES - ROT_DIM // 2, 1),
                                    pltpu.roll(th, ROT_DIM // 2, 1))
                r = th * cos + partner * sin
                outs.append(r * scale if scale != 1.0 else r)
            return jnp.concatenate(outs, axis=1)

        q_ref[0, rows, :] = rope(proj(4), HEAD_DIM ** -0.5 * math.log2(math.e)).astype(bf16)
        k_ref[0, rows, :] = rope(proj(5), 1.0).astype(bf16)
        az = proj(7)
        gate_ref[0, rows, :] = (az * jax.nn.sigmoid(az)).astype(bf16)
        vt = lax.dot_general(wvt_ref[...], u, _NT, preferred_element_type=jnp.float32).astype(bf16)
        for h in range(N_HEADS):
            vt_ref[0, r0 // sub, h * V_EXT:h * V_EXT + V_DIM, :] = vt[h * V_DIM:(h + 1) * V_DIM, :]
            vt_ref[0, r0 // sub, h * V_EXT + V_DIM:(h + 1) * V_EXT, :] = ones_rows

    ubuf_ref[...] = prev


def _projection(x, pos3, g, w32, cw, cb, invf):
    b, s, d = x.shape
    tm = PROJ_TM
    tok = lambda bi, si: (bi, si, 0)
    out_tok = jax.ShapeDtypeStruct((b, s, GROUP), jnp.bfloat16)
    vt_rows = N_HEADS * V_EXT
    return pl.pallas_call(
        functools.partial(_proj_kernel, tm=tm, sub=PROJ_SUB),
        grid=(b, s // tm),
        in_specs=[
            pl.BlockSpec((1, tm, d), tok),
            pl.BlockSpec((1, 1, tm), lambda bi, si: (bi, 0, si)),
            _const_spec((1, d)),
            _const_spec(w32.shape),
            _const_spec(cw.shape),
            _const_spec(cb.shape),
            _const_spec(invf.shape),
        ],
        out_specs=[
            pl.BlockSpec((1, tm, GROUP), tok),
            pl.BlockSpec((1, tm, GROUP), tok),
            pl.BlockSpec((1, tm, GROUP), tok),
            pl.BlockSpec((1, tm // PROJ_SUB, vt_rows, PROJ_SUB), lambda bi, si: (bi, si, 0, 0)),
            pl.BlockSpec((1, tm, GROUP), tok),
        ],
        out_shape=[out_tok, out_tok, out_tok,
                   jax.ShapeDtypeStruct((b, s // PROJ_SUB, vt_rows, PROJ_SUB), jnp.bfloat16), out_tok],
        scratch_shapes=[
            pltpu.VMEM(w32.shape, jnp.bfloat16),
            pltpu.VMEM((GROUP, d), jnp.bfloat16),
            pltpu.VMEM((SUBLANES, GROUP), jnp.float32),
        ],
        compiler_params=pltpu.CompilerParams(
            dimension_semantics=("arbitrary", "arbitrary"), vmem_limit_bytes=VMEM_LIMIT),
        name="proj_conv_rope",
    )(x, pos3, g, w32, cw, cb, invf)


def _attn_kernel(q_ref, qn_ref, k_ref, kn_ref, vt_ref, gate_ref, lam_ref, g_ref, y_ref,
                 s_ref, mt_ref, qz_ref, acc_ref, m_ref, *, tq, tk, lam_init):
    qt = pl.program_id(1)
    first_step = jnp.logical_and(pl.program_id(0) == 0, qt == 0)
    n_chain = 2 * N_HEADS
    heads = [slice(h * LANES, (h + 1) * LANES) for h in range(N_HEADS)]
    lane = lax.broadcasted_iota(jnp.int32, (1, LANES), 1)
    every = slice(0, tq)
    upper = slice(tq // 2, tq)

    def tail_mask(key_tile, cols=every):
        n = cols.stop - cols.start
        kchunk = lax.broadcasted_iota(jnp.int32, (tk, n), 0) // CHUNK + key_tile * (tk // CHUNK)
        qchunk = (lax.broadcasted_iota(jnp.int32, (tk, n), 1) + cols.start) // CHUNK
        return kchunk <= qchunk

    def split_queries(src_ref):
        for h, hs in enumerate(heads):
            qh = src_ref[0, :, hs]
            zero = jnp.zeros_like(qh)
            qz_ref[2 * h] = jnp.where(lane < HEAD_DIM, qh, zero)
            qz_ref[2 * h + 1] = jnp.where(lane >= HEAD_DIM, qh, zero)

    acc_ref[...] = jnp.zeros_like(acc_ref)
    m_ref[...] = jnp.full_like(m_ref, NEG)

    def qk(j, c, slot, mask, cols=every, keys_ref=None):
        if keys_ref is None:
            kh = k_ref[0, pl.ds(pl.multiple_of(j * tk, tk), tk), heads[c // 2]]
        else:
            kh = keys_ref[0, :, heads[c // 2]]
        st = lax.dot_general(kh, qz_ref[c, cols, :], _NT, preferred_element_type=jnp.float32)
        if mask is not None:
            st = jnp.where(mask, st, NEG)
        s_ref[slot, c, :, cols] = st
        mt_ref[slot, c, :, cols] = jnp.max(st, axis=0, keepdims=True)

    def softmax_pv(j, c, slot, cols=every):
        m_old = m_ref[c, :, cols]
        m_new = jnp.maximum(m_old, mt_ref[slot, c, :, cols])
        alpha = jnp.exp2(m_old - m_new)
        p = jnp.exp2(s_ref[slot, c, :, cols] - m_new)
        m_ref[c, :, cols] = m_new
        h = c // 2
        vh = vt_ref[0, j, h * V_EXT:(h + 1) * V_EXT, :]
        acc_ref[c, :, cols] = alpha * acc_ref[c, :, cols] + jnp.dot(
            vh, p.astype(jnp.bfloat16), preferred_element_type=jnp.float32)

    lp = lam_ref[...]
    lam = (jnp.exp(jnp.sum(lp[0:1] * lp[1:2], axis=1, keepdims=True))
           - jnp.exp(jnp.sum(lp[2:3] * lp[3:4], axis=1, keepdims=True)) + lam_init)

    def finalize(h):
        hs = heads[h]
        a1, a2 = acc_ref[2 * h], acc_ref[2 * h + 1]
        o = (a1[:V_DIM] * (1.0 / a1[V_DIM:V_DIM + 1])
             - a2[:V_DIM] * (lam / a2[V_DIM:V_DIM + 1]))
        on = o * lax.rsqrt(jnp.mean(o * o, axis=0, keepdims=True) + SUBLN_EPS)
        y = on.T * (g_ref[...] * (1.0 - lam_init)) * gate_ref[0, :, hs].astype(jnp.float32)
        y_ref[0, :, hs] = y.astype(jnp.bfloat16)

    def visible_stage(j, slot):
        for c in range(n_chain):
            for half in (slice(0, tq // 2), upper):
                qk(j + 1, c, 1 - slot, None, half)
                softmax_pv(j, c, slot, half)

    def stage(j, slot, has_next=True, next_mask=None, cols=every, next_cols=every, last=False):
        if last:
            split_queries(qn_ref)
        for c in range(n_chain):
            if has_next:
                qk(j + 1, c, 1 - slot, next_mask, next_cols)
            softmax_pv(j, c, slot, cols)
            if last and c % 2 == 1:
                finalize(c // 2)
            if last:
                qk(0, c, 1 - slot, None, keys_ref=kn_ref)

    @pl.when(first_step)
    def _():
        split_queries(q_ref)
        for c in range(n_chain):
            qk(0, c, 0, None)

    @pl.when(qt == 0)
    def _():
        mask = tail_mask(0)
        for c in range(n_chain):
            st = jnp.where(mask, s_ref[0, c], NEG)
            s_ref[0, c] = st
            mt_ref[0, c] = jnp.max(st, axis=0, keepdims=True)

    n_pairs = jnp.maximum(qt - 1, 0)

    def quad(i, carry):
        for t in range(4):
            visible_stage(4 * i + t, t % 2)
        return carry

    lax.fori_loop(0, n_pairs // 2, quad, 0)

    @pl.when(n_pairs % 2 == 1)
    def _():
        visible_stage(2 * n_pairs - 2, 0)
        visible_stage(2 * n_pairs - 1, 1)

    def diagonal_stages():
        stage(2 * qt, 0, next_mask=tail_mask(1, upper), next_cols=upper)
        stage(2 * qt + 1, 1, has_next=False, cols=upper, last=True)

    @pl.when(qt > 0)
    def _():
        stage(2 * qt - 2, 0)
        stage(2 * qt - 1, 1, next_mask=tail_mask(0))
        diagonal_stages()

    @pl.when(qt == 0)
    def _():
        diagonal_stages()


def _attention(q, k, vt, gate, lam_params, subln_g, lam_init):
    b, s, w = q.shape
    tq, tk = ATTN_TQ, ATTN_TK
    assert tq == 2 * tk and tk % CHUNK == 0 and vt.shape[3] == tk
    n_chain = 2 * N_HEADS
    f32 = jnp.float32
    nq = s // tq
    qtile = lambda bi, qi: (bi, qi, 0)
    next_b = lambda bi, qi: jnp.minimum(bi + (qi + 1) // nq, b - 1)
    return pl.pallas_call(
        functools.partial(_attn_kernel, tq=tq, tk=tk, lam_init=lam_init),
        grid=(b, nq),
        in_specs=[
            pl.BlockSpec((1, tq, w), qtile),
            pl.BlockSpec((1, tq, w), lambda bi, qi: (next_b(bi, qi), (qi + 1) % nq, 0)),
            pl.BlockSpec((1, s, w), lambda bi, qi: (bi, 0, 0)),
            pl.BlockSpec((1, tk, w), lambda bi, qi: (next_b(bi, qi), 0, 0)),
            pl.BlockSpec((1,) + vt.shape[1:], lambda bi, qi: (bi, 0, 0, 0)),
            pl.BlockSpec((1, tq, w), qtile),
            _const_spec(lam_params.shape),
            _const_spec(subln_g.shape),
        ],
        out_specs=pl.BlockSpec((1, tq, w), qtile),
        out_shape=jax.ShapeDtypeStruct((b, s, w), jnp.bfloat16),
        scratch_shapes=[
            pltpu.VMEM((2, n_chain, tk, tq), f32),
            pltpu.VMEM((2, n_chain, 1, tq), f32),
            pltpu.VMEM((n_chain, tq, LANES), jnp.bfloat16),
            pltpu.VMEM((n_chain, V_EXT, tq), f32),
            pltpu.VMEM((n_chain, 1, tq), f32),
        ],
        compiler_params=pltpu.CompilerParams(
            dimension_semantics=("arbitrary", "arbitrary"), vmem_limit_bytes=VMEM_LIMIT),
        name="diff_attention",
    )(q, q, k, k, vt, gate, lam_params, subln_g)


def _rms(h, g):
    return h * lax.rsqrt(jnp.mean(h * h, axis=-1, keepdims=True) + EPS) * g


def _out_kernel(x_ref, yc_ref, ya_ref, p_ref, wo_ref, wg_ref, wp_ref, fn_ref, o_ref):
    f32, bf16 = jnp.float32, jnp.bfloat16
    tm = x_ref.shape[0]
    halves = [slice(r0, r0 + OUT_SUB) for r0 in range(0, tm, OUT_SUB)]
    mix = [jnp.dot(yc_ref[rows, :], wo_ref[0:CONV_WIDTH, :], preferred_element_type=f32)
           + jnp.dot(ya_ref[rows, :], wo_ref[CONV_WIDTH:, :], preferred_element_type=f32)
           for rows in halves]
    h, z, pp = [], [], []
    for rows, m in zip(halves, mix):
        h.append(x_ref[rows, :] + m)
        z.append(jnp.dot(h[-1].astype(bf16), wg_ref[...], preferred_element_type=f32))
        pp.append(jnp.dot(p_ref[rows, :].astype(bf16), wp_ref[...], preferred_element_type=f32))
    for rows, hh, zz, pr in zip(halves, h, z, pp):
        inv = lax.rsqrt(jnp.mean(hh * hh, axis=-1, keepdims=True) + EPS)
        o_ref[rows, :] = _rms(hh + jax.nn.sigmoid(zz * inv) * pr, fn_ref[...])


def _output(x2, yc2, ya2, p2, wo_bf, wg_bf, wp_bf, final_norm):
    t, d = x2.shape
    tm = OUT_TM
    row = lambda i: (i, 0)
    return pl.pallas_call(
        _out_kernel,
        grid=(t // tm,),
        in_specs=[
            pl.BlockSpec((tm, d), row),
            pl.BlockSpec((tm, CONV_WIDTH), row),
            pl.BlockSpec((tm, ATTN_WIDTH), row),
            pl.BlockSpec((tm, PLE_DIM), row),
            _const_spec(wo_bf.shape),
            _const_spec(wg_bf.shape),
            _const_spec(wp_bf.shape),
            _const_spec(final_norm.shape),
        ],
        out_specs=pl.BlockSpec((tm, d), row),
        out_shape=jax.ShapeDtypeStruct((t, d), jnp.float32),
        compiler_params=pltpu.CompilerParams(
            dimension_semantics=("parallel",), vmem_limit_bytes=VMEM_LIMIT),
        name="out_ple_norm",
    )(x2, yc2, ya2, p2, wo_bf, wg_bf, wp_bf, final_norm)


def kernel(x, p, positions, norm_mix, w_in, conv_w, conv_b, lambda_q1, lambda_k1, lambda_q2,
           lambda_k2, subln_g, w_out, norm_ple, w_ple_gate, w_ple_proj, final_norm):
    b, s, d = x.shape
    depth = p.shape[0]
    assert depth == 1 and d == D_MODEL and w_in.shape[-1] == 8 * GROUP
    assert s % PROJ_TM == 0 and s % ATTN_TQ == 0 and (b * s) % OUT_TM == 0
    bf16 = jnp.bfloat16
    lam_init = 0.8 - 0.6 * math.exp(-0.3 * 0)

    half = ROT_DIM // 2
    invf = (ROPE_THETA ** (-jnp.arange(half, dtype=jnp.float32) / half))[:, None]
    yconv, q, k, vt, gate = _projection(
        x, positions[:, None, :], norm_mix[0][None, :], w_in[0],
        conv_w[0], conv_b[0][None, :], invf)

    lam_params = jnp.stack([lambda_q1[0], lambda_k1[0], lambda_q2[0], lambda_k2[0]])
    yattn = _attention(q, k, vt, gate, lam_params, subln_g[0][None, :], lam_init)

    out = _output(
        x.reshape(b * s, d), yconv.reshape(b * s, CONV_WIDTH), yattn.reshape(b * s, ATTN_WIDTH),
        p[0].reshape(b * s, PLE_DIM), w_out[0].astype(bf16),
        (norm_ple[0][:, None] * w_ple_gate[0]).astype(bf16),
        w_ple_proj[0].astype(bf16), final_norm[None, :])
    return out.reshape(b, s, d)
```
